```python
import math, functools
import jax, jax.numpy as jnp
from jax import lax
import numpy as np

D_MODEL = 1024
BATCH = 4
SEQ = 4096
DEPTH = 2
DEC_BATCH = 32
DEC_SEQ = 1
PAST_LEN = 8192
PAGE_SIZE = 128

N_HEADS = 8
HEAD_DIM = D_MODEL // (2 * N_HEADS)
V_DIM = 2 * HEAD_DIM
ATT_WIDTH = N_HEADS * V_DIM
GM_GROUPS = 8
GM_WIDTH = D_MODEL
GM_GROUP_DIM = GM_WIDTH // GM_GROUPS
CHUNK = 128
D_FF = 4 * D_MODEL
Q_BLOCK = 128
EPS = 1e-6
IN_WIDTH = 3 * ATT_WIDTH + 2 * GM_WIDTH + 2 * D_MODEL

kernel_name = "diffattn_gmlp_gated_hybrid_step"


def rms_norm(x, g):
    xf = x.astype(jnp.float32)
    y = xf * lax.rsqrt(jnp.mean(xf * xf, axis=-1, keepdims=True) + EPS)
    return (y * g.astype(jnp.float32)).astype(x.dtype)


def layer_norm(x, g, b):
    xf = x.astype(jnp.float32)
    xc = xf - jnp.mean(xf, axis=-1, keepdims=True)
    y = xc * lax.rsqrt(jnp.mean(xc * xc, axis=-1, keepdims=True) + EPS)
    return (y * g.astype(jnp.float32) + b.astype(jnp.float32)).astype(x.dtype)


def alibi_slopes():
    return jnp.exp2(-8.0 / N_HEADS * jnp.arange(1, N_HEADS + 1, dtype=jnp.float32))


def diff_scores(q, k, q_pos, k_pos):
    s = jnp.einsum('bqhmd,bkhmd->bhmqk', q.astype(jnp.float32), k.astype(jnp.float32)) * (HEAD_DIM ** -0.5)
    dist = (q_pos[:, None] - k_pos[None, :]).astype(jnp.float32)
    s = s - alibi_slopes()[:, None, None, None] * dist
    return jnp.where(dist >= 0, s, -jnp.inf)


def diff_weights(s, lam):
    p = jax.nn.softmax(s, axis=-1)
    return p[:, :, 0] - lam * p[:, :, 1]


def prompt_attention(q, k, v, lam):
    B, T = q.shape[0], q.shape[1]
    nb = T // Q_BLOCK
    pos = jnp.arange(T, dtype=jnp.int32)
    qb = jnp.moveaxis(q.reshape(B, nb, Q_BLOCK, N_HEADS, 2, HEAD_DIM), 1, 0)
    pb = pos.reshape(nb, Q_BLOCK)
    vf = v.astype(jnp.float32)

    def block(args):
        q_blk, p_blk = args
        w = diff_weights(diff_scores(q_blk, k, p_blk, pos), lam)
        return jnp.einsum('bhqk,bkhe->bqhe', w, vf)

    o = lax.map(block, (qb, pb))
    return jnp.moveaxis(o, 0, 1).reshape(B, T, N_HEADS, V_DIM).astype(q.dtype)


def sample_attention(q, k, v, lam, cache_k_l, cache_v_l, page_table):
    DB, Tn = q.shape[0], q.shape[1]
    past = page_table.shape[1] * PAGE_SIZE
    past_k = cache_k_l[page_table].reshape(DB, past, N_HEADS, 2, HEAD_DIM)
    past_v = cache_v_l[page_table].reshape(DB, past, N_HEADS, V_DIM)
    q_pos = past + jnp.arange(Tn, dtype=jnp.int32)
    s = jnp.concatenate([diff_scores(q, past_k, q_pos, jnp.arange(past, dtype=jnp.int32)),
                         diff_scores(q, k, q_pos, q_pos)], axis=-1)
    w = diff_weights(s, lam)
    o = (jnp.einsum('bhqk,bkhe->bqhe', w[..., :past], past_v.astype(jnp.float32))
         + jnp.einsum('bhqk,bkhe->bqhe', w[..., past:], v.astype(jnp.float32)))
    return o.astype(q.dtype)


def chunk_spatial_gate(z, ln_g, ln_b, ws, bs):
    u, v = jnp.split(z, 2, axis=-1)
    v = layer_norm(v, ln_g, ln_b)
    B, T = v.shape[0], v.shape[1]
    nc = -(-T // CHUNK)
    vp = jnp.pad(v, ((0, 0), (0, nc * CHUNK - T), (0, 0))).reshape(B, nc, CHUNK, GM_GROUPS, GM_GROUP_DIM)
    wm = jnp.where(jnp.tril(jnp.ones((CHUNK, CHUNK), dtype=bool)), ws, jnp.zeros_like(ws))
    s = jnp.einsum('gts,bcsgd->bctgd', wm, vp) + bs.T[None, None, :, :, None]
    s = s.reshape(B, nc * CHUNK, GM_WIDTH)[:, :T]
    return u * s, v


def trunk_layer(x, attn_fn, lambda_init, norm1_g, w_in, lam_q1, lam_k1, lam_q2, lam_k2, subln_g,
                gm_ln_g, gm_ln_b, gm_ws, gm_b, w_pa, w_pb, w_o, norm2_g, w_up, w_down):
    B, T = x.shape[0], x.shape[1]
    h = rms_norm(x, norm1_g)
    q, k, v, z, gates = jnp.split(h @ w_in, [ATT_WIDTH, 2 * ATT_WIDTH, 3 * ATT_WIDTH,
                                             3 * ATT_WIDTH + 2 * GM_WIDTH], axis=-1)
    q = q.reshape(B, T, N_HEADS, 2, HEAD_DIM)
    k = k.reshape(B, T, N_HEADS, 2, HEAD_DIM)
    v = v.reshape(B, T, N_HEADS, V_DIM)
    lam = (jnp.exp(jnp.sum(lam_q1.astype(jnp.float32) * lam_k1.astype(jnp.float32)))
           - jnp.exp(jnp.sum(lam_q2.astype(jnp.float32) * lam_k2.astype(jnp.float32))) + lambda_init)
    o = attn_fn(q, k, v, lam)
    o = rms_norm(o, subln_g) * (1.0 - lambda_init)
    a = o.reshape(B, T, ATT_WIDTH) @ w_pa
    s, gv = chunk_spatial_gate(jax.nn.gelu(z), gm_ln_g, gm_ln_b, gm_ws, gm_b)
    b = s @ w_pb
    g_a, g_b = jnp.split(jax.nn.sigmoid(gates), 2, axis=-1)
    x = x + (g_a * a + g_b * b) @ w_o
    h2 = rms_norm(x, norm2_g)
    x = x + jnp.square(jax.nn.relu(h2 @ w_up)) @ w_down
    return x, k, v, gv


def setup_inputs(seed: int = 0) -> dict:
    key = jax.random.key(seed)
    ks = jax.random.split(key, 24)
    n_pages = PAST_LEN // PAGE_SIZE
    n_pool = (DEC_BATCH * n_pages * 5) // 4

    def nrm(k, shape, scale):
        return jax.random.normal(k, shape, jnp.float32) * scale

    page_table = jax.random.permutation(ks[4], n_pool)[:DEC_BATCH * n_pages].reshape(DEC_BATCH, n_pages).astype(jnp.int32)
    return {
        "x_prompt": nrm(ks[0], (BATCH, SEQ, D_MODEL), 1.0),
        "x_sample": nrm(ks[1], (DEC_BATCH, DEC_SEQ, D_MODEL), 1.0),
        "cache_k": nrm(ks[2], (DEPTH, n_pool, PAGE_SIZE, N_HEADS, 2, HEAD_DIM), 1.0),
        "cache_v": nrm(ks[3], (DEPTH, n_pool, PAGE_SIZE, N_HEADS, V_DIM), 1.0),
        "page_table": page_table,
        "norm1_g": 1.0 + nrm(ks[5], (DEPTH, D_MODEL), 0.02),
        "w_in": nrm(ks[6], (DEPTH, D_MODEL, IN_WIDTH), D_MODEL ** -0.5),
        "lam_q1": nrm(ks[7], (DEPTH, HEAD_DIM), 0.1),
        "lam_k1": nrm(ks[8], (DEPTH, HEAD_DIM), 0.1),
        "lam_q2": nrm(ks[9], (DEPTH, HEAD_DIM), 0.1),
        "lam_k2": nrm(ks[10], (DEPTH, HEAD_DIM), 0.1),
        "subln_g": 1.0 + nrm(ks[11], (DEPTH, V_DIM), 0.02),
        "gm_ln_g": 1.0 + nrm(ks[12], (DEPTH, GM_WIDTH), 0.02),
        "gm_ln_b": nrm(ks[13], (DEPTH, GM_WIDTH), 0.02),
        "gm_ws": nrm(ks[14], (DEPTH, GM_GROUPS, CHUNK, CHUNK), CHUNK ** -0.5),
        "gm_b": 1.0 + nrm(ks[15], (DEPTH, GM_GROUPS, CHUNK), 0.02),
        "w_pa": nrm(ks[16], (DEPTH, ATT_WIDTH, D_MODEL), ATT_WIDTH ** -0.5),
        "w_pb": nrm(ks[17], (DEPTH, GM_WIDTH, D_MODEL), GM_WIDTH ** -0.5),
        "w_o": nrm(ks[18], (DEPTH, D_MODEL, D_MODEL), D_MODEL ** -0.5),
        "norm2_g": 1.0 + nrm(ks[19], (DEPTH, D_MODEL), 0.02),
        "w_up": nrm(ks[20], (DEPTH, D_MODEL, D_FF), D_MODEL ** -0.5),
        "w_down": nrm(ks[21], (DEPTH, D_FF, D_MODEL), D_FF ** -0.5),
        "final_g": 1.0 + nrm(ks[22], (D_MODEL,), 0.02),
    }


def reference(x_prompt, x_sample, cache_k, cache_v, page_table, norm1_g, w_in, lam_q1, lam_k1, lam_q2,
              lam_k2, subln_g, gm_ln_g, gm_ln_b, gm_ws, gm_b, w_pa, w_pb, w_o, norm2_g, w_up, w_down,
              final_g):
    xp, xs = x_prompt, x_sample
    kp, vp, ksm, vsm, gvs = [], [], [], [], []
    for l in range(DEPTH):
        lambda_init = 0.8 - 0.6 * math.exp(-0.3 * l)
        params = (norm1_g[l], w_in[l], lam_q1[l], lam_k1[l], lam_q2[l], lam_k2[l], subln_g[l],
                  gm_ln_g[l], gm_ln_b[l], gm_ws[l], gm_b[l], w_pa[l], w_pb[l], w_o[l], norm2_g[l],
                  w_up[l], w_down[l])
        xp, k_l, v_l, _ = trunk_layer(xp, prompt_attention, lambda_init, *params)
        attn_s = functools.partial(sample_attention, cache_k_l=cache_k[l], cache_v_l=cache_v[l],
                                   page_table=page_table)
        xs, ks_l, vs_l, gv_l = trunk_layer(xs, attn_s, lambda_init, *params)
        kp.append(k_l)
        vp.append(v_l)
        ksm.append(ks_l)
        vsm.append(vs_l)
        gvs.append(gv_l)
    y_prompt = rms_norm(xp, final_g)
    y_sample = rms_norm(xs, final_g)
    return (y_prompt, y_sample, jnp.stack(kp), jnp.stack(vp), jnp.stack(ksm), jnp.stack(vsm), jnp.stack(gvs))
```

```python
import functools
import math

import jax
import jax.numpy as jnp
from jax import lax
from jax.experimental import pallas as pl
from jax.experimental.pallas import tpu as pltpu

D_MODEL = 1024
N_HEADS = 8
HEAD_DIM = 64
V_DIM = 128
GM_GROUPS = 8
CHUNK = 128
PAGE_SIZE = 128
EPS = 1e-6

F32 = jnp.float32
BF16 = jnp.bfloat16

ROW_TILE = 256
Q_TILE = 256
K_TILE = 256
PAGES_PER_STEP = 8
VMEM_LIMIT = 56 * 1024 * 1024


def _const_spec(shape):
    nd = len(shape)
    return pl.BlockSpec(shape, lambda *_: (0,) * nd, pipeline_mode=pl.Buffered(1))


def _params(n_axes):
    return pltpu.CompilerParams(dimension_semantics=("arbitrary",) * n_axes, vmem_limit_bytes=VMEM_LIMIT)


def _gelu_tanh(x):
    c = math.sqrt(2.0 / math.pi)
    return 0.5 * x * (1.0 + jnp.tanh(c * (x + 0.044715 * (x * x * x))))


def _rms(x, g):
    return x * lax.rsqrt(jnp.mean(x * x, axis=-1, keepdims=True) + EPS) * g


def _lambda(lamv, lambda_init):
    s1 = jnp.sum(lamv[0:1] * lamv[1:2], axis=-1, keepdims=True)
    s2 = jnp.sum(lamv[2:3] * lamv[3:4], axis=-1, keepdims=True)
    return jnp.exp(s1) - jnp.exp(s2) + lambda_init


def _nt_dot(a, b):
    return lax.dot_general(a, b, (((1,), (1,)), ((), ())), preferred_element_type=F32)


def _inproj_kernel(decode, n_alias, x_ref, g1_ref, win_ref, wkt_ref, lng_ref, lnb_ref, ws_ref, bs_ref, *rest):
    outs = rest[n_alias:]
    if decode:
        q_ref, kf_ref, vf_ref, bin_ref, ga_ref, gb_ref, gv_ref = outs
    else:
        q_ref, kt_ref, ktb_ref, vf_ref, vb_ref, bin_ref, ga_ref, gb_ref = outs
    rows = x_ref.shape[0]
    h = _rms(x_ref[...], g1_ref[...]).astype(BF16)

    def proj(c):
        return jnp.dot(h, win_ref[:, c * D_MODEL:(c + 1) * D_MODEL], preferred_element_type=F32)

    q = proj(0)
    v = proj(2)
    if decode:
        q_ref[...] = q
        kf_ref[...] = proj(1)
    else:
        q_ref[...] = (q * (HEAD_DIM ** -0.5)).astype(BF16)
        kt = _nt_dot(wkt_ref[...], h)
        kt_ref[0, 0] = kt
        ktb_ref[0] = kt.astype(BF16)
        vb_ref[...] = v.astype(BF16)
    for hd in range(N_HEADS):
        if decode:
            vf_ref[:, hd, :] = v[:, hd * V_DIM:(hd + 1) * V_DIM]
        else:
            vf_ref[0, 0, :, hd, :] = v[:, hd * V_DIM:(hd + 1) * V_DIM]

    ga_ref[...] = jax.nn.sigmoid(proj(5)).astype(BF16)
    gb_ref[...] = jax.nn.sigmoid(proj(6)).astype(BF16)

    zv = _gelu_tanh(proj(4))
    xc = zv - jnp.mean(zv, axis=-1, keepdims=True)
    vg = xc * lax.rsqrt(jnp.mean(xc * xc, axis=-1, keepdims=True) + EPS) * lng_ref[...] + lnb_ref[...]
    u = _gelu_tanh(proj(3))

    if decode:
        gv_ref[...] = vg
        bin_ref[...] = (u * (ws_ref[...] * vg + bs_ref[...])).astype(BF16)
    else:
        vgb = vg.astype(BF16)
        tril = (lax.broadcasted_iota(jnp.int32, (CHUNK, CHUNK), 0)
                >= lax.broadcasted_iota(jnp.int32, (CHUNK, CHUNK), 1))
        for g in range(GM_GROUPS):
            wm = jnp.where(tril, ws_ref[g], 0.0).astype(BF16)
            cols = slice(g * CHUNK, (g + 1) * CHUNK)
            for c in range(rows // CHUNK):
                rs = slice(c * CHUNK, (c + 1) * CHUNK)
                s = jnp.dot(wm, vgb[rs, cols], preferred_element_type=F32) + bs_ref[g]
                bin_ref[rs, cols] = (u[rs, cols] * s).astype(BF16)


def _inproj_prompt(x, g1, win, wkt, lng, lnb, ws, bs, kt_all, v_all, *, layer, depth, batch, seq):
    n = batch * seq
    tm = ROW_TILE
    nt = seq // tm
    row = lambda dt: jax.ShapeDtypeStruct((n, D_MODEL), dt)
    out_shape = [row(BF16),
                 jax.ShapeDtypeStruct((depth, batch, D_MODEL, seq), F32),
                 jax.ShapeDtypeStruct((batch, D_MODEL, seq), BF16),
                 jax.ShapeDtypeStruct((depth, batch, seq, N_HEADS, V_DIM), F32),
                 row(BF16), row(BF16), row(BF16), row(BF16)]
    tile = pl.BlockSpec((tm, D_MODEL), lambda b, i: (b * nt + i, 0))
    out_specs = [tile,
                 pl.BlockSpec((1, 1, D_MODEL, tm), lambda b, i: (layer, b, 0, i)),
                 pl.BlockSpec((1, D_MODEL, tm), lambda b, i: (b, 0, i)),
                 pl.BlockSpec((1, 1, tm, N_HEADS, V_DIM), lambda b, i: (layer, b, i, 0, 0)),
                 tile, tile, tile, tile]
    consts = [g1, win, wkt, lng, lnb, ws, bs]
    in_specs = [tile] + [_const_spec(c.shape) for c in consts]
    args = [x] + consts
    aliases = {}
    n_alias = 0
    if kt_all is not None:
        in_specs += [pl.BlockSpec(memory_space=pl.ANY)] * 2
        aliases = {len(args): 1, len(args) + 1: 3}
        args += [kt_all, v_all]
        n_alias = 2
    return pl.pallas_call(
        functools.partial(_inproj_kernel, False, n_alias),
        out_shape=out_shape, grid=(batch, nt), in_specs=in_specs, out_specs=out_specs,
        input_output_aliases=aliases, compiler_params=_params(2), name="inproj_prompt",
    )(*args)


def _inproj_decode(x, g1, win, wkt, lng, lnb, ws, bs):
    n = x.shape[0]
    row = lambda dt: jax.ShapeDtypeStruct((n, D_MODEL), dt)
    out_shape = [row(F32), row(F32), jax.ShapeDtypeStruct((n, N_HEADS, V_DIM), F32),
                 row(BF16), row(BF16), row(BF16), row(F32)]
    consts = [g1, win, wkt, lng, lnb, ws, bs]
    full = lambda s: pl.BlockSpec(s.shape, lambda i: (0,) * len(s.shape))
    return pl.pallas_call(
        functools.partial(_inproj_kernel, True, 0),
        out_shape=out_shape, grid=(1,),
        in_specs=[full(x)] + [_const_spec(c.shape) for c in consts],
        out_specs=[full(s) for s in out_shape],
        compiler_params=_params(1), name="inproj_decode",
    )(x, *consts)


def _output_kernel(final, x_ref, a_ref, b_ref, ga_ref, gb_ref, wpa_ref, wpb_ref, wo_ref, g2_ref,
                   wup_ref, wdn_ref, gf_ref, o_ref):
    a = jnp.dot(a_ref[...], wpa_ref[...], preferred_element_type=F32)
    b = jnp.dot(b_ref[...], wpb_ref[...], preferred_element_type=F32)
    mix = ga_ref[...].astype(F32) * a + gb_ref[...].astype(F32) * b
    x = x_ref[...] + jnp.dot(mix.astype(BF16), wo_ref[...], preferred_element_type=F32)
    h2 = _rms(x, g2_ref[...]).astype(BF16)
    hid = jnp.maximum(jnp.dot(h2, wup_ref[...], preferred_element_type=F32), 0.0)
    x = x + jnp.dot((hid * hid).astype(BF16), wdn_ref[...], preferred_element_type=F32)
    o_ref[...] = _rms(x, gf_ref[...]) if final else x


def _output(x, a_in, b_in, ga, gb, wpa, wpb, wo, g2, wup, wdn, gf, *, final, tm):
    n = x.shape[0]
    tile = pl.BlockSpec((tm, D_MODEL), lambda i: (i, 0))
    consts = [wpa, wpb, wo, g2, wup, wdn, gf]
    return pl.pallas_call(
        functools.partial(_output_kernel, final),
        out_shape=jax.ShapeDtypeStruct((n, D_MODEL), F32),
        grid=(n // tm,),
        in_specs=[tile] * 5 + [_const_spec(c.shape) for c in consts],
        out_specs=tile, compiler_params=_params(1), name="output_block",
    )(x, a_in, b_in, ga, gb, *consts)


def _prompt_attn_kernel(lambda_init, q_ref, kt_ref, v_ref, lamv_ref, subg_ref, o_ref):
    hd = pl.program_id(1)
    qi = pl.program_id(2)
    tq, tk = Q_TILE, K_TILE
    q = q_ref[...]
    lane = lax.broadcasted_iota(jnp.int32, q.shape, 1)
    zero = jnp.zeros_like(q)
    qs = jnp.concatenate([jnp.where(lane < HEAD_DIM, q, zero), jnp.where(lane >= HEAD_DIM, q, zero)], axis=0)
    slope = jnp.exp2(-(hd + 1).astype(F32) * jnp.ones((1, tk), F32))
    kcol = lax.broadcasted_iota(jnp.int32, (1, tk), 1)
    q0 = qi * tq

    def block(j, carry, masked):
        m, l, acc = carry
        k0 = pl.multiple_of(j * tk, tk)
        s = jnp.dot(qs, kt_ref[0, :, pl.ds(k0, tk)], preferred_element_type=F32)
        s = s + slope * (kcol + (k0 - q0)).astype(F32)
        if masked:
            qrow = lax.broadcasted_iota(jnp.int32, (2 * tq, tk), 0)
            qrow = jnp.where(qrow >= tq, qrow - tq, qrow)
            s = jnp.where(qrow >= lax.broadcasted_iota(jnp.int32, (2 * tq, tk), 1), s, -jnp.inf)
        m_new = jnp.maximum(m, jnp.max(s, axis=-1, keepdims=True))
        alpha = jnp.exp(m - m_new)
        p = jnp.exp(s - m_new)
        l = alpha * l + jnp.sum(p, axis=-1, keepdims=True)
        acc = alpha * acc + jnp.dot(p.astype(BF16), v_ref[pl.ds(k0, tk), :], preferred_element_type=F32)
        return m_new, l, acc

    init = (jnp.full((2 * tq, 1), -jnp.inf, F32), jnp.zeros((2 * tq, 1), F32),
            jnp.zeros((2 * tq, V_DIM), F32))
    carry = lax.fori_loop(0, qi, lambda j, c: block(j, c, False), init)
    m, l, acc = block(qi, carry, True)
    o = acc / l
    lam = _lambda(lamv_ref[...], lambda_init)
    o = o[:tq] - lam * o[tq:]
    o_ref[...] = (_rms(o, subg_ref[...]) * (1.0 - lambda_init)).astype(BF16)


def _prompt_attention(q, kt, v, lamv, subg, *, batch, seq, lambda_init):
    assert Q_TILE == K_TILE
    nq = seq // Q_TILE
    qspec = pl.BlockSpec((Q_TILE, V_DIM), lambda b, h, i: (b * nq + i, h))
    return pl.pallas_call(
        functools.partial(_prompt_attn_kernel, lambda_init),
        out_shape=jax.ShapeDtypeStruct(q.shape, BF16),
        grid=(batch, N_HEADS, nq),
        in_specs=[qspec,
                  pl.BlockSpec((1, V_DIM, seq), lambda b, h, i: (b, h, 0)),
                  pl.BlockSpec((seq, V_DIM), lambda b, h, i: (b, h)),
                  _const_spec(lamv.shape), _const_spec(subg.shape)],
        out_specs=qspec, compiler_params=_params(3), name="prompt_attention",
    )(q, kt, v, lamv, subg)


def _decode_attn_kernel(lambda_init, n_pages, pt_ref, q_ref, kn_ref, vn_ref, lamv_ref, subg_ref, *rest):
    npp = PAGES_PER_STEP
    k_refs, v_refs = rest[:npp], rest[npp:2 * npp]
    o_ref, qcol_ref, s0_ref, s1_ref, self_ref, w_ref, wself_ref, acc_ref = rest[2 * npp:]
    step = pl.program_id(1)
    half = n_pages // npp
    past = n_pages * PAGE_SIZE
    slope = jnp.exp2(-(lax.broadcasted_iota(jnp.int32, (N_HEADS, 1), 0) + 1).astype(F32))

    @pl.when(step == 0)
    def _():
        q = q_ref[0] * (HEAD_DIM ** -0.5)
        qcol_ref[...] = jnp.broadcast_to(q, (V_DIM, D_MODEL)).T
        row = lax.broadcasted_iota(jnp.int32, (2 * N_HEADS, D_MODEL), 0)
        col = lax.broadcasted_iota(jnp.int32, (2 * N_HEADS, D_MODEL), 1)
        own = (col // V_DIM == row % N_HEADS) & ((col // HEAD_DIM) % 2 == row // N_HEADS)
        prod = jnp.broadcast_to(q * kn_ref[0], (2 * N_HEADS, D_MODEL))
        self_ref[...] = jnp.sum(jnp.where(own, prod, 0.0), axis=-1, keepdims=True)

    @pl.when(step < half)
    def _():
        qcol = qcol_ref[...]
        for i, kr in enumerate(k_refs):
            pr = (kr[0, 0] * qcol).reshape(N_HEADS, 2, HEAD_DIM, PAGE_SIZE)
            page = step * npp + i
            kpos = page * PAGE_SIZE + lax.broadcasted_iota(jnp.int32, (1, PAGE_SIZE), 1)
            bias = slope * (kpos - past).astype(F32)
            off = pl.multiple_of(page * PAGE_SIZE, PAGE_SIZE)
            s0_ref[:, pl.ds(off, PAGE_SIZE)] = jnp.sum(pr[:, 0], axis=1) + bias
            s1_ref[:, pl.ds(off, PAGE_SIZE)] = jnp.sum(pr[:, 1], axis=1) + bias

    @pl.when(step == half)
    def _():
        lam = _lambda(lamv_ref[...], lambda_init)

        def soft(s_ref, s_self):
            s = s_ref[...]
            m = jnp.maximum(jnp.max(s, axis=-1, keepdims=True), s_self)
            p = jnp.exp(s - m)
            p_self = jnp.exp(s_self - m)
            inv = 1.0 / (jnp.sum(p, axis=-1, keepdims=True) + p_self)
            return p * inv, p_self * inv

        w0, w0s = soft(s0_ref, self_ref[:N_HEADS])
        w1, w1s = soft(s1_ref, self_ref[N_HEADS:])
        w_ref[...] = w0 - lam * w1
        wself_ref[...] = w0s - lam * w1s
        acc_ref[...] = jnp.zeros_like(acc_ref)

    @pl.when(step >= half)
    def _():
        lane = lax.broadcasted_iota(jnp.int32, (N_HEADS, PAGE_SIZE), 1)
        keep = (lane % N_HEADS) == lax.broadcasted_iota(jnp.int32, (N_HEADS, PAGE_SIZE), 0)
        per = PAGE_SIZE // N_HEADS
        acc = acc_ref[...]
        for i, vr in enumerate(v_refs):
            off = pl.multiple_of(((step - half) * npp + i) * PAGE_SIZE, PAGE_SIZE)
            w = w_ref[:, pl.ds(off, PAGE_SIZE)]
            wsel = jnp.concatenate(
                [jnp.where(keep, jnp.take_along_axis(w, b * per + lane // N_HEADS, axis=1), 0.0)
                 for b in range(N_HEADS)], axis=1).astype(BF16)
            vflat = vr[0, 0].reshape(PAGE_SIZE * N_HEADS, V_DIM).astype(BF16)
            acc = acc + jnp.dot(wsel, vflat, preferred_element_type=F32)
        acc_ref[...] = acc

    @pl.when(step == pl.num_programs(1) - 1)
    def _():
        o = acc_ref[...] + wself_ref[...] * vn_ref[0]
        o_ref[0] = (_rms(o, subg_ref[...]) * (1.0 - lambda_init)).astype(BF16)


def _decode_attention(q, kn, vn, cache_kt, cache_v, page_table, lamv, subg, *, layer, lambda_init):
    nb, n_pages = page_table.shape
    npp = PAGES_PER_STEP
    half = n_pages // npp
    past = n_pages * PAGE_SIZE
    vec = pl.BlockSpec((1, 1, D_MODEL), lambda b, s, pt: (b, 0, 0))
    heads = pl.BlockSpec((1, N_HEADS, V_DIM), lambda b, s, pt: (b, 0, 0))

    def k_spec(i):
        return pl.BlockSpec((1, 1, D_MODEL, PAGE_SIZE),
                            lambda b, s, pt: (layer, pt[b, jnp.minimum(s, half - 1) * npp + i], 0, 0))

    def v_spec(i):
        return pl.BlockSpec((1, 1, PAGE_SIZE, N_HEADS, V_DIM),
                            lambda b, s, pt: (layer, pt[b, jnp.maximum(s - half, 0) * npp + i], 0, 0, 0))

    grid_spec = pltpu.PrefetchScalarGridSpec(
        num_scalar_prefetch=1,
        grid=(nb, 2 * half),
        in_specs=[vec, vec, heads,
                  pl.BlockSpec(lamv.shape, lambda b, s, pt: (0, 0)),
                  pl.BlockSpec(subg.shape, lambda b, s, pt: (0, 0))]
                 + [k_spec(i) for i in range(npp)] + [v_spec(i) for i in range(npp)],
        out_specs=heads,
        scratch_shapes=[pltpu.VMEM((D_MODEL, PAGE_SIZE), F32),
                        pltpu.VMEM((N_HEADS, past), F32),
                        pltpu.VMEM((N_HEADS, past), F32),
                        pltpu.VMEM((2 * N_HEADS, 1), F32),
                        pltpu.VMEM((N_HEADS, past), F32),
                        pltpu.VMEM((N_HEADS, 1), F32),
                        pltpu.VMEM((N_HEADS, V_DIM), F32)],
    )
    return pl.pallas_call(
        functools.partial(_decode_attn_kernel, lambda_init, n_pages),
        out_shape=jax.ShapeDtypeStruct((nb, N_HEADS, V_DIM), BF16),
        grid_spec=grid_spec, compiler_params=_params(2), name="decode_attention",
    )(page_table, q[:, None], kn[:, None], vn, lamv, subg,
      *([cache_kt] * npp), *([cache_v] * npp))


def kernel(x_prompt, x_sample, cache_k, cache_v, page_table, norm1_g, w_in, lam_q1, lam_k1, lam_q2, lam_k2,
           subln_g, gm_ln_g, gm_ln_b, gm_ws, gm_b, w_pa, w_pb, w_o, norm2_g, w_up, w_down, final_g):
    batch, seq, _ = x_prompt.shape
    nb = x_sample.shape[0]
    depth = w_in.shape[0]
    n_pool = cache_k.shape[1]
    xp = x_prompt.reshape(batch * seq, D_MODEL)
    xs = x_sample.reshape(nb, D_MODEL)
    ckt = jnp.transpose(cache_k, (0, 1, 3, 4, 5, 2)).reshape(depth, n_pool, D_MODEL, PAGE_SIZE)
    rowv = lambda a: a.reshape(1, -1)
    gf = rowv(final_g)

    kt_all = v_all = None
    ksm, vsm, gvs = [], [], []
    for l in range(depth):
        lambda_init = 0.8 - 0.6 * math.exp(-0.3 * l)
        last = l == depth - 1
        win = w_in[l].astype(BF16)
        wkt = w_in[l][:, D_MODEL:2 * D_MODEL].T.astype(BF16)
        wpa, wpb, wo = w_pa[l].astype(BF16), w_pb[l].astype(BF16), w_o[l].astype(BF16)
        wup, wdn = w_up[l].astype(BF16), w_down[l].astype(BF16)
        g1, g2 = rowv(norm1_g[l]), rowv(norm2_g[l])
        lng, lnb = rowv(gm_ln_g[l]), rowv(gm_ln_b[l])
        lamv = jnp.stack([lam_q1[l], lam_k1[l], lam_q2[l], lam_k2[l]])
        subg = rowv(subln_g[l])

        bs_full = jnp.broadcast_to(gm_b[l][:, :, None], (GM_GROUPS, CHUNK, CHUNK))
        q, kt_all, ktb, v_all, vb, b_in, ga, gb = _inproj_prompt(
            xp, g1, win, wkt, lng, lnb, gm_ws[l], bs_full, kt_all, v_all,
            layer=l, depth=depth, batch=batch, seq=seq)
        a_in = _prompt_attention(q, ktb, vb, lamv, subg, batch=batch, seq=seq, lambda_init=lambda_init)
        xp = _output(xp, a_in, b_in, ga, gb, wpa, wpb, wo, g2, wup, wdn, gf, final=last, tm=ROW_TILE)

        ws_row = rowv(jnp.repeat(gm_ws[l][:, 0, 0], CHUNK))
        bs_row = rowv(jnp.repeat(gm_b[l][:, 0], CHUNK))
        qs_, kfs, vfs, b_in_s, ga_s, gb_s, gv = _inproj_decode(xs, g1, win, wkt, lng, lnb, ws_row, bs_row)
        a_in_s = _decode_attention(qs_, kfs, vfs, ckt, cache_v, page_table, lamv, subg,
                                   layer=l, lambda_init=lambda_init)
        xs = _output(xs, a_in_s.reshape(nb, D_MODEL), b_in_s, ga_s, gb_s, wpa, wpb, wo, g2, wup, wdn, gf,
                     final=last, tm=nb)
        ksm.append(kfs)
        vsm.append(vfs)
        gvs.append(gv)

    y_prompt = xp.reshape(batch, seq, D_MODEL)
    y_sample = xs.reshape(nb, 1, D_MODEL)
    k_prompt = jnp.transpose(kt_all.reshape(depth, batch, N_HEADS, 2, HEAD_DIM, seq), (0, 1, 5, 2, 3, 4))
    k_sample = jnp.stack(ksm).reshape(depth, nb, 1, N_HEADS, 2, HEAD_DIM)
    v_sample = jnp.stack(vsm).reshape(depth, nb, 1, N_HEADS, V_DIM)
    gv_sample = jnp.stack(gvs).reshape(depth, nb, 1, D_MODEL)
    return y_prompt, y_sample, k_prompt, v_all, k_sample, v_sample, gv_sample
```

```python
import functools
import math

import jax
import jax.numpy as jnp
from jax import lax
from jax.experimental import pallas as pl
from jax.experimental.pallas import tpu as pltpu

D_MODEL = 1024
N_HEADS = 8
HEAD_DIM = 64
V_DIM = 128
GM_GROUPS = 8
CHUNK = 128
PAGE_SIZE = 128
EPS = 1e-6

F32 = jnp.float32
BF16 = jnp.bfloat16

ROW_TILE = 256
Q_TILE = 512
K_TILE = 256
HEADS_PER_STEP = 2
AUG = 16
LOG2E = math.log2(math.e)
PAGES_PER_STEP = 8
VMEM_LIMIT = 56 * 1024 * 1024


def _const_spec(shape):
    nd = len(shape)
    return pl.BlockSpec(shape, lambda *_: (0,) * nd, pipeline_mode=pl.Buffered(1))


def _params(n_axes):
    return pltpu.CompilerParams(dimension_semantics=("arbitrary",) * n_axes, vmem_limit_bytes=VMEM_LIMIT)


def _gelu_tanh(x):
    c = math.sqrt(2.0 / math.pi)
    return 0.5 * x * (1.0 + jnp.tanh(c * (x + 0.044715 * (x * x * x))))


def _rms(x, g):
    return x * lax.rsqrt(jnp.mean(x * x, axis=-1, keepdims=True) + EPS) * g


def _lambda(lamv, lambda_init):
    s1 = jnp.sum(lamv[0:1] * lamv[1:2], axis=-1, keepdims=True)
    s2 = jnp.sum(lamv[2:3] * lamv[3:4], axis=-1, keepdims=True)
    return jnp.exp(s1) - jnp.exp(s2) + lambda_init


def _inproj_kernel(decode, n_alias, x_ref, g1_ref, win_ref, lng_ref, lnb_ref, ws_ref, bs_ref, *rest):
    outs = rest[n_alias:]
    if decode:
        q_ref, kf_ref, vf_ref, bin_ref, ga_ref, gb_ref, gv_ref = outs
    else:
        qt_ref, kt_ref, kb_ref, vf_ref, vt_ref, bin_ref, ga_ref, gb_ref = outs
    rows = x_ref.shape[0]
    h = _rms(x_ref[...], g1_ref[...]).astype(BF16)

    def proj(c):
        return jnp.dot(h, win_ref[:, c * D_MODEL:(c + 1) * D_MODEL], preferred_element_type=F32)

    q = proj(0)
    k = proj(1)
    v = proj(2)
    if decode:
        q_ref[...] = q
        kf_ref[...] = k
    else:
        qt_ref[0] = (q * (LOG2E * HEAD_DIM ** -0.5)).T.astype(BF16)
        kt_ref[0, 0] = k.T
        kb_ref[...] = k.astype(BF16)
        vt_ref[0] = v.T.astype(BF16)
    for hd in range(N_HEADS):
        if decode:
            vf_ref[:, hd, :] = v[:, hd * V_DIM:(hd + 1) * V_DIM]
        else:
            vf_ref[0, 0, :, hd, :] = v[:, hd * V_DIM:(hd + 1) * V_DIM]

    ga_ref[...] = jax.nn.sigmoid(proj(5)).astype(BF16)
    gb_ref[...] = jax.nn.sigmoid(proj(6)).astype(BF16)

    zv = _gelu_tanh(proj(4))
    xc = zv - jnp.mean(zv, axis=-1, keepdims=True)
    vg = xc * lax.rsqrt(jnp.mean(xc * xc, axis=-1, keepdims=True) + EPS) * lng_ref[...] + lnb_ref[...]
    u = _gelu_tanh(proj(3))

    if decode:
        gv_ref[...] = vg
        bin_ref[...] = (u * (ws_ref[...] * vg + bs_ref[...])).astype(BF16)
    else:
        vgb = vg.astype(BF16)
        tril = (lax.broadcasted_iota(jnp.int32, (CHUNK, CHUNK), 0)
                >= lax.broadcasted_iota(jnp.int32, (CHUNK, CHUNK), 1))
        for g in range(GM_GROUPS):
            wm = jnp.where(tril, ws_ref[g], 0.0).astype(BF16)
            cols = slice(g * CHUNK, (g + 1) * CHUNK)
            for c in range(rows // CHUNK):
                rs = slice(c * CHUNK, (c + 1) * CHUNK)
                s = jnp.dot(wm, vgb[rs, cols], preferred_element_type=F32) + bs_ref[g]
                bin_ref[rs, cols] = (u[rs, cols] * s).astype(BF16)


def _inproj_prompt(x, g1, win, lng, lnb, ws, bs, kt_all, v_all, *, layer, depth, batch, seq):
    n = batch * seq
    tm = ROW_TILE
    nt = seq // tm
    row = jax.ShapeDtypeStruct((n, D_MODEL), BF16)
    fmaj = jax.ShapeDtypeStruct((batch, D_MODEL, seq), BF16)
    out_shape = [fmaj,
                 jax.ShapeDtypeStruct((depth, batch, D_MODEL, seq), F32),
                 row,
                 jax.ShapeDtypeStruct((depth, batch, seq, N_HEADS, V_DIM), F32),
                 fmaj, row, row, row]
    tile = pl.BlockSpec((tm, D_MODEL), lambda b, i: (b * nt + i, 0))
    ftile = pl.BlockSpec((1, D_MODEL, tm), lambda b, i: (b, 0, i))
    out_specs = [ftile,
                 pl.BlockSpec((1, 1, D_MODEL, tm), lambda b, i: (layer, b, 0, i)),
                 tile,
                 pl.BlockSpec((1, 1, tm, N_HEADS, V_DIM), lambda b, i: (layer, b, i, 0, 0)),
                 ftile, tile, tile, tile]
    consts = [g1, win, lng, lnb, ws, bs]
    in_specs = [tile] + [_const_spec(c.shape) for c in consts]
    args = [x] + consts
    aliases = {}
    n_alias = 0
    if kt_all is not None:
        in_specs += [pl.BlockSpec(memory_space=pl.ANY)] * 2
        aliases = {len(args): 1, len(args) + 1: 3}
        args += [kt_all, v_all]
        n_alias = 2
    return pl.pallas_call(
        functools.partial(_inproj_kernel, False, n_alias),
        out_shape=out_shape, grid=(batch, nt), in_specs=in_specs, out_specs=out_specs,
        input_output_aliases=aliases, compiler_params=_params(2), name="inproj_prompt",
    )(*args)


def _inproj_decode(x, g1, win, lng, lnb, ws, bs):
    n = x.shape[0]
    row = lambda dt: jax.ShapeDtypeStruct((n, D_MODEL), dt)
    out_shape = [row(F32), row(F32), jax.ShapeDtypeStruct((n, N_HEADS, V_DIM), F32),
                 row(BF16), row(BF16), row(BF16), row(F32)]
    consts = [g1, win, lng, lnb, ws, bs]
    full = lambda s: pl.BlockSpec(s.shape, lambda i: (0,) * len(s.shape))
    return pl.pallas_call(
        functools.partial(_inproj_kernel, True, 0),
        out_shape=out_shape, grid=(1,),
        in_specs=[full(x)] + [_const_spec(c.shape) for c in consts],
        out_specs=[full(s) for s in out_shape],
        compiler_params=_params(1), name="inproj_decode",
    )(x, *consts)


def _output_kernel(final, a_fmajor, x_ref, a_ref, b_ref, ga_ref, gb_ref, wpa_ref, wpb_ref, wo_ref, g2_ref,
                   wup_ref, wdn_ref, gf_ref, o_ref):
    if a_fmajor:
        a = lax.dot_general(a_ref[0], wpa_ref[...], (((0,), (0,)), ((), ())), preferred_element_type=F32)
    else:
        a = jnp.dot(a_ref[...], wpa_ref[...], preferred_element_type=F32)
    b = jnp.dot(b_ref[...], wpb_ref[...], preferred_element_type=F32)
    mix = ga_ref[...].astype(F32) * a + gb_ref[...].astype(F32) * b
    x = x_ref[...] + jnp.dot(mix.astype(BF16), wo_ref[...], preferred_element_type=F32)
    h2 = _rms(x, g2_ref[...]).astype(BF16)
    hid = jnp.maximum(jnp.dot(h2, wup_ref[...], preferred_element_type=F32), 0.0)
    x = x + jnp.dot((hid * hid).astype(BF16), wdn_ref[...], preferred_element_type=F32)
    o_ref[...] = _rms(x, gf_ref[...]) if final else x


def _output(x, a_in, b_in, ga, gb, wpa, wpb, wo, g2, wup, wdn, gf, *, final, tm):
    n = x.shape[0]
    tile = pl.BlockSpec((tm, D_MODEL), lambda i: (i, 0))
    a_fmajor = a_in.ndim == 3
    if a_fmajor:
        nt = a_in.shape[2] // tm
        a_spec = pl.BlockSpec((1, D_MODEL, tm), lambda i: (i // nt, 0, i % nt))
    else:
        a_spec = tile
    consts = [wpa, wpb, wo, g2, wup, wdn, gf]
    return pl.pallas_call(
        functools.partial(_output_kernel, final, a_fmajor),
        out_shape=jax.ShapeDtypeStruct((n, D_MODEL), F32),
        grid=(n // tm,),
        in_specs=[tile, a_spec, tile, tile, tile] + [_const_spec(c.shape) for c in consts],
        out_specs=tile, compiler_params=_params(1), name="output_block",
    )(x, a_in, b_in, ga, gb, *consts)


def _prompt_attn_kernel(lambda_init, q_ref, k_ref, vt_ref, lamv_ref, subg_ref, o_ref,
                        sa_scr, sb_scr, m_scr, l_scr, acc_scr):
    hp = pl.program_id(1)
    qi = pl.program_id(2)
    tq, tk = Q_TILE, K_TILE
    q0 = qi * tq
    kcol = lax.broadcasted_iota(jnp.int32, (tk, V_DIM), 1)
    ramp = jnp.where(kcol < 2, lax.broadcasted_iota(jnp.int32, (tk, V_DIM), 0), 0).astype(F32).astype(BF16)
    ones = jnp.ones((AUG, tk), BF16)
    feat = lax.broadcasted_iota(jnp.int32, (V_DIM, tq), 0)

    chains = []
    for hh in range(HEADS_PER_STEP):
        hd = hp * HEADS_PER_STEP + hh
        qt = q_ref[0, hh * V_DIM:(hh + 1) * V_DIM, :]
        slope = jnp.exp2(-(hd + 1).astype(F32) * jnp.ones((1, tq), F32)) * LOG2E
        s_hi = slope.astype(BF16).astype(F32)
        s_lo = (slope - s_hi).astype(BF16).astype(F32)
        aug_rows = jnp.where(feat == 0, s_hi, jnp.where(feat == 1, s_lo, 0.0)).astype(BF16)
        for mp in range(2):
            keep = (feat < HEAD_DIM) if mp == 0 else (feat >= HEAD_DIM)
            w = jnp.concatenate([jnp.where(keep, qt, jnp.zeros_like(qt)), aug_rows], axis=0)
            chains.append((hh, w, s_hi + s_lo))

    nch = len(chains)

    def scores(j, ci, buf):
        hh, w, _ = chains[ci]
        k0 = pl.multiple_of(j * tk, tk)
        kaug = jnp.concatenate([k_ref[pl.ds(k0, tk), hh * V_DIM:(hh + 1) * V_DIM], ramp], axis=1)
        buf[ci] = jnp.dot(kaug, w, preferred_element_type=F32)

    def update(j, ci, buf, masked):
        hh, _, slope_v = chains[ci]
        s = buf[ci]
        k0 = pl.multiple_of(j * tk, tk)
        c = slope_v * (k0 - q0).astype(F32)
        if masked:
            kk = lax.broadcasted_iota(jnp.int32, (tk, tq), 0) + (k0 - q0)
            qq = lax.broadcasted_iota(jnp.int32, (tk, tq), 1)
            s = jnp.where(qq >= kk, s, -jnp.inf)
        m = m_scr[ci]
        m_new = jnp.maximum(m, jnp.max(s, axis=0, keepdims=True) + c)
        alpha = jnp.exp2(m - m_new)
        m_scr[ci] = m_new
        p = jnp.exp2(s - (m_new - c)).astype(BF16)
        vaug = jnp.concatenate([vt_ref[0, hh * V_DIM:(hh + 1) * V_DIM, pl.ds(k0, tk)], ones], axis=0)
        pv = jnp.dot(vaug, p, preferred_element_type=F32)
        l_scr[ci] = alpha * l_scr[ci] + pv[V_DIM:V_DIM + 8]
        acc_scr[ci] = alpha * acc_scr[ci] + pv[:V_DIM]

    def stage(j_next, buf_next, j, buf, masked):
        if j_next is not None:
            for ci in range(nch):
                scores(j_next, ci, buf_next)
        for ci in range(nch):
            update(j, ci, buf, masked)

    def body(t, carry):
        stage(2 * t + 1, sb_scr, 2 * t, sa_scr, False)
        stage(2 * t + 2, sa_scr, 2 * t + 1, sb_scr, False)
        return carry

    m_scr[...] = jnp.full(m_scr.shape, -jnp.inf, F32)
    l_scr[...] = jnp.zeros(l_scr.shape, F32)
    acc_scr[...] = jnp.zeros(acc_scr.shape, F32)
    for ci in range(nch):
        scores(0, ci, sa_scr)
    lax.fori_loop(0, qi, body, 0)
    stage(2 * qi + 1, sb_scr, 2 * qi, sa_scr, True)
    stage(None, None, 2 * qi + 1, sb_scr, True)

    lam = _lambda(lamv_ref[...], lambda_init)
    subg = jnp.concatenate([subg_ref[...]] * (tq // V_DIM), axis=1)
    for hh in range(HEADS_PER_STEP):
        o0 = acc_scr[2 * hh] / l_scr[2 * hh, 0:1]
        o1 = acc_scr[2 * hh + 1] / l_scr[2 * hh + 1, 0:1]
        ot = o0 - lam * o1
        ot = ot * lax.rsqrt(jnp.mean(ot * ot, axis=0, keepdims=True) + EPS) * subg
        o_ref[0, hh * V_DIM:(hh + 1) * V_DIM, :] = (ot * (1.0 - lambda_init)).astype(BF16)


def _prompt_attention(qt, kb, vt, lamv, subg_tile, *, batch, seq, lambda_init):
    assert Q_TILE == 2 * K_TILE and seq % Q_TILE == 0
    hps = HEADS_PER_STEP
    fspec = pl.BlockSpec((1, hps * V_DIM, Q_TILE), lambda b, h, i: (b, h, i))
    nch = 2 * hps
    return pl.pallas_call(
        functools.partial(_prompt_attn_kernel, lambda_init),
        out_shape=jax.ShapeDtypeStruct(qt.shape, BF16),
        grid=(batch, N_HEADS // hps, seq // Q_TILE),
        in_specs=[fspec,
                  pl.BlockSpec((seq, hps * V_DIM), lambda b, h, i: (b, h)),
                  pl.BlockSpec((1, hps * V_DIM, seq), lambda b, h, i: (b, h, 0)),
                  _const_spec(lamv.shape), _const_spec(subg_tile.shape)],
        out_specs=fspec, compiler_params=_params(3), name="prompt_attention",
        scratch_shapes=[pltpu.VMEM((nch, K_TILE, Q_TILE), F32), pltpu.VMEM((nch, K_TILE, Q_TILE), F32),
                        pltpu.VMEM((nch, 1, Q_TILE), F32), pltpu.VMEM((nch, 8, Q_TILE), F32),
                        pltpu.VMEM((nch, V_DIM, Q_TILE), F32)],
    )(qt, kb, vt, lamv, subg_tile)


def _decode_attn_kernel(lambda_init, n_pages, pt_ref, q_ref, kn_ref, vn_ref, lamv_ref, subg_ref, *rest):
    npp = PAGES_PER_STEP
    k_refs, v_refs = rest[:npp], rest[npp:2 * npp]
    o_ref, qcol_ref, s0_ref, s1_ref, self_ref, w_ref, wself_ref, acc_ref = rest[2 * npp:]
    step = pl.program_id(1)
    half = n_pages // npp
    past = n_pages * PAGE_SIZE
    slope = jnp.exp2(-(lax.broadcasted_iota(jnp.int32, (N_HEADS, 1), 0) + 1).astype(F32))

    @pl.when(step == 0)
    def _():
        q = q_ref[0] * (HEAD_DIM ** -0.5)
        qcol_ref[...] = jnp.broadcast_to(q, (V_DIM, D_MODEL)).T
        row = lax.broadcasted_iota(jnp.int32, (2 * N_HEADS, D_MODEL), 0)
        col = lax.broadcasted_iota(jnp.int32, (2 * N_HEADS, D_MODEL), 1)
        own = (col // V_DIM == row % N_HEADS) & ((col // HEAD_DIM) % 2 == row // N_HEADS)
        prod = jnp.broadcast_to(q * kn_ref[0], (2 * N_HEADS, D_MODEL))
        self_ref[...] = jnp.sum(jnp.where(own, prod, 0.0), axis=-1, keepdims=True)

    @pl.when(step < half)
    def _():
        qcol = qcol_ref[...]
        for i, kr in enumerate(k_refs):
            pr = (kr[0, 0] * qcol).reshape(N_HEADS, 2, HEAD_DIM, PAGE_SIZE)
            page = step * npp + i
            kpos = page * PAGE_SIZE + lax.broadcasted_iota(jnp.int32, (1, PAGE_SIZE), 1)
            bias = slope * (kpos - past).astype(F32)
            off = pl.multiple_of(page * PAGE_SIZE, PAGE_SIZE)
            s0_ref[:, pl.ds(off, PAGE_SIZE)] = jnp.sum(pr[:, 0], axis=1) + bias
            s1_ref[:, pl.ds(off, PAGE_SIZE)] = jnp.sum(pr[:, 1], axis=1) + bias

    @pl.when(step == half)
    def _():
        lam = _lambda(lamv_ref[...], lambda_init)

        def soft(s_ref, s_self):
            s = s_ref[...]
            m = jnp.maximum(jnp.max(s, axis=-1, keepdims=True), s_self)
            p = jnp.exp(s - m)
            p_self = jnp.exp(s_self - m)
            inv = 1.0 / (jnp.sum(p, axis=-1, keepdims=True) + p_self)
            return p * inv, p_self * inv

        w0, w0s = soft(s0_ref, self_ref[:N_HEADS])
        w1, w1s = soft(s1_ref, self_ref[N_HEADS:])
        w_ref[...] = w0 - lam * w1
        wself_ref[...] = w0s - lam * w1s
        acc_ref[...] = jnp.zeros_like(acc_ref)

    @pl.when(step >= half)
    def _():
        lane = lax.broadcasted_iota(jnp.int32, (N_HEADS, PAGE_SIZE), 1)
        keep = (lane % N_HEADS) == lax.broadcasted_iota(jnp.int32, (N_HEADS, PAGE_SIZE), 0)
        per = PAGE_SIZE // N_HEADS
        acc = acc_ref[...]
        for i, vr in enumerate(v_refs):
            off = pl.multiple_of(((step - half) * npp + i) * PAGE_SIZE, PAGE_SIZE)
            w = w_ref[:, pl.ds(off, PAGE_SIZE)]
            wsel = jnp.concatenate(
                [jnp.where(keep, jnp.take_along_axis(w, b * per + lane // N_HEADS, axis=1), 0.0)
                 for b in range(N_HEADS)], axis=1).astype(BF16)
            vflat = vr[0, 0].reshape(PAGE_SIZE * N_HEADS, V_DIM).astype(BF16)
            acc = acc + jnp.dot(wsel, vflat, preferred_element_type=F32)
        acc_ref[...] = acc

    @pl.when(step == pl.num_programs(1) - 1)
    def _():
        o = acc_ref[...] + wself_ref[...] * vn_ref[0]
        o_ref[0] = (_rms(o, subg_ref[...]) * (1.0 - lambda_init)).astype(BF16)


def _decode_attention(q, kn, vn, cache_kt, cache_v, page_table, lamv, subg, *, layer, lambda_init):
    nb, n_pages = page_table.shape
    npp = PAGES_PER_STEP
    half = n_pages // npp
    past = n_pages * PAGE_SIZE
    vec = pl.BlockSpec((1, 1, D_MODEL), lambda b, s, pt: (b, 0, 0))
    heads = pl.BlockSpec((1, N_HEADS, V_DIM), lambda b, s, pt: (b, 0, 0))

    def k_spec(i):
        return pl.BlockSpec((1, 1, D_MODEL, PAGE_SIZE),
                            lambda b, s, pt: (layer, pt[b, jnp.minimum(s, half - 1) * npp + i], 0, 0))

    def v_spec(i):
        return pl.BlockSpec((1, 1, PAGE_SIZE, N_HEADS, V_DIM),
                            lambda b, s, pt: (layer, pt[b, jnp.maximum(s - half, 0) * npp + i], 0, 0, 0))

    grid_spec = pltpu.PrefetchScalarGridSpec(
        num_scalar_prefetch=1,
        grid=(nb, 2 * half),
        in_specs=[vec, vec, heads,
                  pl.BlockSpec(lamv.shape, lambda b, s, pt: (0, 0)),
                  pl.BlockSpec(subg.shape, lambda b, s, pt: (0, 0))]
                 + [k_spec(i) for i in range(npp)] + [v_spec(i) for i in range(npp)],
        out_specs=heads,
        scratch_shapes=[pltpu.VMEM((D_MODEL, PAGE_SIZE), F32),
                        pltpu.VMEM((N_HEADS, past), F32),
                        pltpu.VMEM((N_HEADS, past), F32),
                        pltpu.VMEM((2 * N_HEADS, 1), F32),
                        pltpu.VMEM((N_HEADS, past), F32),
                        pltpu.VMEM((N_HEADS, 1), F32),
                        pltpu.VMEM((N_HEADS, V_DIM), F32)],
    )
    return pl.pallas_call(
        functools.partial(_decode_attn_kernel, lambda_init, n_pages),
        out_shape=jax.ShapeDtypeStruct((nb, N_HEADS, V_DIM), BF16),
        grid_spec=grid_spec, compiler_params=_params(2), name="decode_attention",
    )(page_table, q[:, None], kn[:, None], vn, lamv, subg,
      *([cache_kt] * npp), *([cache_v] * npp))


def kernel(x_prompt, x_sample, cache_k, cache_v, page_table, norm1_g, w_in, lam_q1, lam_k1, lam_q2, lam_k2,
           subln_g, gm_ln_g, gm_ln_b, gm_ws, gm_b, w_pa, w_pb, w_o, norm2_g, w_up, w_down, final_g):
    batch, seq, _ = x_prompt.shape
    nb = x_sample.shape[0]
    depth = w_in.shape[0]
    n_pool = cache_k.shape[1]
    xp = x_prompt.reshape(batch * seq, D_MODEL)
    xs = x_sample.reshape(nb, D_MODEL)
    ckt = jnp.transpose(cache_k, (0, 1, 3, 4, 5, 2)).reshape(depth, n_pool, D_MODEL, PAGE_SIZE)
    rowv = lambda a: a.reshape(1, -1)
    gf = rowv(final_g)

    kt_all = v_all = None
    ksm, vsm, gvs = [], [], []
    for l in range(depth):
        lambda_init = 0.8 - 0.6 * math.exp(-0.3 * l)
        last = l == depth - 1
        win = w_in[l].astype(BF16)
        wpa, wpb, wo = w_pa[l].astype(BF16), w_pb[l].astype(BF16), w_o[l].astype(BF16)
        wup, wdn = w_up[l].astype(BF16), w_down[l].astype(BF16)
        g1, g2 = rowv(norm1_g[l]), rowv(norm2_g[l])
        lng, lnb = rowv(gm_ln_g[l]), rowv(gm_ln_b[l])
        lamv = jnp.stack([lam_q1[l], lam_k1[l], lam_q2[l], lam_k2[l]])
        subg = rowv(subln_g[l])

        bs_full = jnp.broadcast_to(gm_b[l][:, :, None], (GM_GROUPS, CHUNK, CHUNK))
        qt, kt_all, kb, v_all, vt, b_in, ga, gb = _inproj_prompt(
            xp, g1, win, lng, lnb, gm_ws[l], bs_full, kt_all, v_all,
            layer=l, depth=depth, batch=batch, seq=seq)
        subg_tile = jnp.broadcast_to(subln_g[l][:, None], (V_DIM, V_DIM))
        a_in = _prompt_attention(qt, kb, vt, lamv, subg_tile, batch=batch, seq=seq, lambda_init=lambda_init)
        xp = _output(xp, a_in, b_in, ga, gb, wpa, wpb, wo, g2, wup, wdn, gf, final=last, tm=ROW_TILE)

        ws_row = rowv(jnp.repeat(gm_ws[l][:, 0, 0], CHUNK))
        bs_row = rowv(jnp.repeat(gm_b[l][:, 0], CHUNK))
        qs_, kfs, vfs, b_in_s, ga_s, gb_s, gv = _inproj_decode(xs, g1, win, lng, lnb, ws_row, bs_row)
        a_in_s = _decode_attention(qs_, kfs, vfs, ckt, cache_v, page_table, lamv, subg,
                                   layer=l, lambda_init=lambda_init)
        xs = _output(xs, a_in_s.reshape(nb, D_MODEL), b_in_s, ga_s, gb_s, wpa, wpb, wo, g2, wup, wdn, gf,
                     final=last, tm=nb)
        ksm.append(kfs)
        vsm.append(vfs)
        gvs.append(gv)

    y_prompt = xp.reshape(batch, seq, D_MODEL)
    y_sample = xs.reshape(nb, 1, D_MODEL)
    k_prompt = jnp.transpose(kt_all.reshape(depth, batch, N_HEADS, 2, HEAD_DIM, seq), (0, 1, 5, 2, 3, 4))
    k_sample = jnp.stack(ksm).reshape(depth, nb, 1, N_HEADS, 2, HEAD_DIM)
    v_sample = jnp.stack(vsm).reshape(depth, nb, 1, N_HEADS, V_DIM)
    gv_sample = jnp.stack(gvs).reshape(depth, nb, 1, D_MODEL)
    return y_prompt, y_sample, k_prompt, v_all, k_sample, v_sample, gv_sample
```

```python
import functools
import math

import jax
import jax.numpy as jnp
from jax import lax
from jax.experimental import pallas as pl
from jax.experimental.pallas import tpu as pltpu

D_MODEL = 1024
N_HEADS = 8
HEAD_DIM = 64
V_DIM = 128
GM_GROUPS = 8
CHUNK = 128
PAGE_SIZE = 128
EPS = 1e-6

F32 = jnp.float32
BF16 = jnp.bfloat16

ROW_TILE = 256
Q_TILE = 512
K_TILE = 256
HEADS_PER_STEP = 2
AUG = 16
KEY_PAGE_GROUP = 2
VALUE_PAGE_GROUP = 8
LOG2E = math.log2(math.e)
VMEM_LIMIT = 56 * 1024 * 1024


def _const_spec(shape):
    nd = len(shape)
    return pl.BlockSpec(shape, lambda *_: (0,) * nd, pipeline_mode=pl.Buffered(1))


def _params(n_axes):
    return pltpu.CompilerParams(dimension_semantics=("arbitrary",) * n_axes, vmem_limit_bytes=VMEM_LIMIT)


def _gelu_tanh(x):
    c = math.sqrt(2.0 / math.pi)
    return 0.5 * x * (1.0 + jnp.tanh(c * (x + 0.044715 * (x * x * x))))


def _rms(x, g):
    return x * lax.rsqrt(jnp.mean(x * x, axis=-1, keepdims=True) + EPS) * g


def _lambda(lamv, lambda_init):
    s1 = jnp.sum(lamv[0:1] * lamv[1:2], axis=-1, keepdims=True)
    s2 = jnp.sum(lamv[2:3] * lamv[3:4], axis=-1, keepdims=True)
    return jnp.exp(s1) - jnp.exp(s2) + lambda_init


def _inproj_kernel(decode, n_alias, x_ref, g1_ref, win_ref, lng_ref, lnb_ref, ws_ref, bs_ref, *rest):
    outs = rest[n_alias:]
    if decode:
        q_ref, kf_ref, vf_ref, bin_ref, ga_ref, gb_ref, gv_ref = outs
    else:
        qt_ref, kt_ref, kb_ref, vf_ref, vt_ref, bin_ref, ga_ref, gb_ref = outs
    rows = x_ref.shape[0]
    h = _rms(x_ref[...], g1_ref[...]).astype(BF16)

    def proj(c):
        return jnp.dot(h, win_ref[:, c * D_MODEL:(c + 1) * D_MODEL], preferred_element_type=F32)

    q = proj(0)
    k = proj(1)
    v = proj(2)
    if decode:
        q_ref[...] = q
        kf_ref[...] = k
    else:
        qt_ref[0] = (q * (LOG2E * HEAD_DIM ** -0.5)).T.astype(BF16)
        kt_ref[0, 0] = k.T
        kb_ref[...] = k.astype(BF16)
        vt_ref[0] = v.T.astype(BF16)
    for hd in range(N_HEADS):
        if decode:
            vf_ref[:, hd, :] = v[:, hd * V_DIM:(hd + 1) * V_DIM]
        else:
            vf_ref[0, 0, :, hd, :] = v[:, hd * V_DIM:(hd + 1) * V_DIM]

    ga_ref[...] = jax.nn.sigmoid(proj(5)).astype(BF16)
    gb_ref[...] = jax.nn.sigmoid(proj(6)).astype(BF16)

    zv = _gelu_tanh(proj(4))
    xc = zv - jnp.mean(zv, axis=-1, keepdims=True)
    vg = xc * lax.rsqrt(jnp.mean(xc * xc, axis=-1, keepdims=True) + EPS) * lng_ref[...] + lnb_ref[...]
    u = _gelu_tanh(proj(3))

    if decode:
        gv_ref[...] = vg
        bin_ref[...] = (u * (ws_ref[...] * vg + bs_ref[...])).astype(BF16)
    else:
        vgb = vg.astype(BF16)
        tril = (lax.broadcasted_iota(jnp.int32, (CHUNK, CHUNK), 0)
                >= lax.broadcasted_iota(jnp.int32, (CHUNK, CHUNK), 1))
        for g in range(GM_GROUPS):
            wm = jnp.where(tril, ws_ref[g], 0.0).astype(BF16)
            cols = slice(g * CHUNK, (g + 1) * CHUNK)
            for c in range(rows // CHUNK):
                rs = slice(c * CHUNK, (c + 1) * CHUNK)
                s = jnp.dot(wm, vgb[rs, cols], preferred_element_type=F32) + bs_ref[g]
                bin_ref[rs, cols] = (u[rs, cols] * s).astype(BF16)


def _inproj_prompt(x, g1, win, lng, lnb, ws, bs, kt_all, v_all, *, layer, depth, batch, seq):
    n = batch * seq
    tm = ROW_TILE
    nt = seq // tm
    row = jax.ShapeDtypeStruct((n, D_MODEL), BF16)
    fmaj = jax.ShapeDtypeStruct((batch, D_MODEL, seq), BF16)
    out_shape = [fmaj,
                 jax.ShapeDtypeStruct((depth, batch, D_MODEL, seq), F32),
                 row,
                 jax.ShapeDtypeStruct((depth, batch, seq, N_HEADS, V_DIM), F32),
                 fmaj, row, row, row]
    tile = pl.BlockSpec((tm, D_MODEL), lambda b, i: (b * nt + i, 0))
    ftile = pl.BlockSpec((1, D_MODEL, tm), lambda b, i: (b, 0, i))
    out_specs = [ftile,
                 pl.BlockSpec((1, 1, D_MODEL, tm), lambda b, i: (layer, b, 0, i)),
                 tile,
                 pl.BlockSpec((1, 1, tm, N_HEADS, V_DIM), lambda b, i: (layer, b, i, 0, 0)),
                 ftile, tile, tile, tile]
    consts = [g1, win, lng, lnb, ws, bs]
    in_specs = [tile] + [_const_spec(c.shape) for c in consts]
    args = [x] + consts
    aliases = {}
    n_alias = 0
    if kt_all is not None:
        in_specs += [pl.BlockSpec(memory_space=pl.ANY)] * 2
        aliases = {len(args): 1, len(args) + 1: 3}
        args += [kt_all, v_all]
        n_alias = 2
    return pl.pallas_call(
        functools.partial(_inproj_kernel, False, n_alias),
        out_shape=out_shape, grid=(batch, nt), in_specs=in_specs, out_specs=out_specs,
        input_output_aliases=aliases, compiler_params=_params(2), name="inproj_prompt",
    )(*args)


def _inproj_decode(x, g1, win, lng, lnb, ws, bs):
    n = x.shape[0]
    row = lambda dt: jax.ShapeDtypeStruct((n, D_MODEL), dt)
    out_shape = [row(F32), row(F32), jax.ShapeDtypeStruct((n, N_HEADS, V_DIM), F32),
                 row(BF16), row(BF16), row(BF16), row(F32)]
    consts = [g1, win, lng, lnb, ws, bs]
    full = lambda s: pl.BlockSpec(s.shape, lambda i: (0,) * len(s.shape))
    return pl.pallas_call(
        functools.partial(_inproj_kernel, True, 0),
        out_shape=out_shape, grid=(1,),
        in_specs=[full(x)] + [_const_spec(c.shape) for c in consts],
        out_specs=[full(s) for s in out_shape],
        compiler_params=_params(1), name="inproj_decode",
    )(x, *consts)


def _output_kernel(final, a_fmajor, x_ref, a_ref, b_ref, ga_ref, gb_ref, wpa_ref, wpb_ref, wo_ref, g2_ref,
                   wup_ref, wdn_ref, gf_ref, o_ref):
    if a_fmajor:
        a = lax.dot_general(a_ref[0], wpa_ref[...], (((0,), (0,)), ((), ())), preferred_element_type=F32)
    else:
        a = jnp.dot(a_ref[...], wpa_ref[...], preferred_element_type=F32)
    b = jnp.dot(b_ref[...], wpb_ref[...], preferred_element_type=F32)
    mix = ga_ref[...].astype(F32) * a + gb_ref[...].astype(F32) * b
    x = x_ref[...] + jnp.dot(mix.astype(BF16), wo_ref[...], preferred_element_type=F32)
    h2 = _rms(x, g2_ref[...]).astype(BF16)
    hid = jnp.maximum(jnp.dot(h2, wup_ref[...], preferred_element_type=F32), 0.0)
    x = x + jnp.dot((hid * hid).astype(BF16), wdn_ref[...], preferred_element_type=F32)
    o_ref[...] = _rms(x, gf_ref[...]) if final else x


def _output(x, a_in, b_in, ga, gb, wpa, wpb, wo, g2, wup, wdn, gf, *, final, tm):
    n = x.shape[0]
    tile = pl.BlockSpec((tm, D_MODEL), lambda i: (i, 0))
    a_fmajor = a_in.ndim == 3
    if a_fmajor:
        nt = a_in.shape[2] // tm
        a_spec = pl.BlockSpec((1, D_MODEL, tm), lambda i: (i // nt, 0, i % nt))
    else:
        a_spec = tile
    consts = [wpa, wpb, wo, g2, wup, wdn, gf]
    return pl.pallas_call(
        functools.partial(_output_kernel, final, a_fmajor),
        out_shape=jax.ShapeDtypeStruct((n, D_MODEL), F32),
        grid=(n // tm,),
        in_specs=[tile, a_spec, tile, tile, tile] + [_const_spec(c.shape) for c in consts],
        out_specs=tile, compiler_params=_params(1), name="output_block",
    )(x, a_in, b_in, ga, gb, *consts)


def _attention_kernel(lambda_init, sample_cfg, pt_ref, q_ref, k_ref, vt_ref, lamv_ref, subg_ref, *rest):
    n_sample_in = 6
    sample_in = rest[:n_sample_in]
    o_ref, os_ref = rest[n_sample_in:n_sample_in + 2]
    sa_scr, sb_scr, m_scr, l_scr, acc_scr = rest[n_sample_in + 2:n_sample_in + 7]
    sample_scr = rest[n_sample_in + 7:]
    hp = pl.program_id(1)
    qi = pl.program_id(2)
    step = (pl.program_id(0) * pl.num_programs(1) + hp) * pl.num_programs(2) + qi
    tq, tk = Q_TILE, K_TILE
    q0 = qi * tq
    kcol = lax.broadcasted_iota(jnp.int32, (tk, V_DIM), 1)
    ramp = jnp.where(kcol < 2, lax.broadcasted_iota(jnp.int32, (tk, V_DIM), 0), 0).astype(F32).astype(BF16)
    ones = jnp.ones((AUG, tk), BF16)
    feat = lax.broadcasted_iota(jnp.int32, (V_DIM, tq), 0)

    chains = []
    for hh in range(HEADS_PER_STEP):
        hd = hp * HEADS_PER_STEP + hh
        qt = q_ref[0, hh * V_DIM:(hh + 1) * V_DIM, :]
        slope = jnp.exp2(-(hd + 1).astype(F32) * jnp.ones((1, tq), F32)) * LOG2E
        s_hi = slope.astype(BF16).astype(F32)
        s_lo = (slope - s_hi).astype(BF16).astype(F32)
        aug_rows = jnp.where(feat == 0, s_hi, jnp.where(feat == 1, s_lo, 0.0)).astype(BF16)
        for mp in range(2):
            keep = (feat < HEAD_DIM) if mp == 0 else (feat >= HEAD_DIM)
            w = jnp.concatenate([jnp.where(keep, qt, jnp.zeros_like(qt)), aug_rows], axis=0)
            chains.append((hh, w, s_hi + s_lo))

    nch = len(chains)

    def scores(j, ci, buf):
        hh, w, _ = chains[ci]
        k0 = pl.multiple_of(j * tk, tk)
        kaug = jnp.concatenate([k_ref[pl.ds(k0, tk), hh * V_DIM:(hh + 1) * V_DIM], ramp], axis=1)
        buf[ci] = jnp.dot(kaug, w, preferred_element_type=F32)

    def update(j, ci, buf, masked):
        hh, _, slope_v = chains[ci]
        s = buf[ci]
        k0 = pl.multiple_of(j * tk, tk)
        c = slope_v * (k0 - q0).astype(F32)
        if masked:
            kk = lax.broadcasted_iota(jnp.int32, (tk, tq), 0) + (k0 - q0)
            qq = lax.broadcasted_iota(jnp.int32, (tk, tq), 1)
            s = jnp.where(qq >= kk, s, -jnp.inf)
        m = m_scr[ci]
        m_new = jnp.maximum(m, jnp.max(s, axis=0, keepdims=True) + c)
        alpha = jnp.exp2(m - m_new)
        m_scr[ci] = m_new
        p = jnp.exp2(s - (m_new - c)).astype(BF16)
        vaug = jnp.concatenate([vt_ref[0, hh * V_DIM:(hh + 1) * V_DIM, pl.ds(k0, tk)], ones], axis=0)
        pv = jnp.dot(vaug, p, preferred_element_type=F32)
        l_scr[ci] = alpha * l_scr[ci] + pv[V_DIM:V_DIM + 8]
        acc_scr[ci] = alpha * acc_scr[ci] + pv[:V_DIM]

    def stage(j_next, buf_next, j, buf, masked):
        if j_next is not None:
            for ci in range(nch):
                scores(j_next, ci, buf_next)
        for ci in range(nch):
            update(j, ci, buf, masked)

    def body(t, carry):
        stage(2 * t + 1, sb_scr, 2 * t, sa_scr, False)
        stage(2 * t + 2, sa_scr, 2 * t + 1, sb_scr, False)
        return carry

    m_scr[...] = jnp.full(m_scr.shape, -jnp.inf, F32)
    l_scr[...] = jnp.zeros(l_scr.shape, F32)
    acc_scr[...] = jnp.zeros(acc_scr.shape, F32)
    for ci in range(nch):
        scores(0, ci, sa_scr)
    _sample_stream(lambda_init, sample_cfg, step, pt_ref, lamv_ref, sample_in, os_ref, sample_scr)
    lax.fori_loop(0, qi, body, 0)
    stage(2 * qi + 1, sb_scr, 2 * qi, sa_scr, True)
    stage(None, None, 2 * qi + 1, sb_scr, True)

    lam = _lambda(lamv_ref[...], lambda_init)
    subg = jnp.concatenate([subg_ref[...]] * (tq // V_DIM), axis=1)
    for hh in range(HEADS_PER_STEP):
        o0 = acc_scr[2 * hh] / l_scr[2 * hh, 0:1]
        o1 = acc_scr[2 * hh + 1] / l_scr[2 * hh + 1, 0:1]
        ot = o0 - lam * o1
        ot = ot * lax.rsqrt(jnp.mean(ot * ot, axis=0, keepdims=True) + EPS) * subg
        o_ref[0, hh * V_DIM:(hh + 1) * V_DIM, :] = (ot * (1.0 - lambda_init)).astype(BF16)


def _attention(qt, kb, vt, lamv, subg_tile, qs, kn, vn, subg_row, cache_kt, cache_vf, page_table, *,
               batch, seq, layer, lambda_init):
    assert Q_TILE == 2 * K_TILE and seq % Q_TILE == 0
    hps = HEADS_PER_STEP
    grid = (batch, N_HEADS // hps, seq // Q_TILE)
    n_steps = grid[0] * grid[1] * grid[2]
    nb, n_pages = page_table.shape
    cfg = _SampleCfg(layer, nb, n_pages, n_steps)
    past = n_pages * PAGE_SIZE
    fspec = pl.BlockSpec((1, hps * V_DIM, Q_TILE), lambda b, h, i, pt: (b, h, i))
    whole = lambda a: pl.BlockSpec(a.shape, lambda b, h, i, pt: (0,) * a.ndim, pipeline_mode=pl.Buffered(1))
    hbm = pl.BlockSpec(memory_space=pl.ANY)
    nch = 2 * hps
    sample_out = jax.ShapeDtypeStruct((nb, N_HEADS, V_DIM), F32)
    grid_spec = pltpu.PrefetchScalarGridSpec(
        num_scalar_prefetch=1, grid=grid,
        in_specs=[fspec,
                  pl.BlockSpec((seq, hps * V_DIM), lambda b, h, i, pt: (b, h)),
                  pl.BlockSpec((1, hps * V_DIM, seq), lambda b, h, i, pt: (b, h, 0)),
                  whole(lamv), whole(subg_tile),
                  whole(qs), whole(kn), whole(vn), whole(subg_row), hbm, hbm],
        out_specs=[fspec, pl.BlockSpec(sample_out.shape, lambda b, h, i, pt: (0, 0, 0))],
        scratch_shapes=[pltpu.VMEM((nch, K_TILE, Q_TILE), F32), pltpu.VMEM((nch, K_TILE, Q_TILE), F32),
                        pltpu.VMEM((nch, 1, Q_TILE), F32), pltpu.VMEM((nch, 8, Q_TILE), F32),
                        pltpu.VMEM((nch, V_DIM, Q_TILE), F32),
                        pltpu.VMEM((cfg.pages_per_step, D_MODEL, PAGE_SIZE), F32),
                        pltpu.SemaphoreType.DMA((cfg.pages_per_step,)),
                        pltpu.VMEM((D_MODEL, PAGE_SIZE), F32),
                        pltpu.VMEM((N_HEADS, past), F32),
                        pltpu.VMEM((N_HEADS, past), F32),
                        pltpu.VMEM((2 * N_HEADS, 1), F32),
                        pltpu.VMEM((N_HEADS, past), F32),
                        pltpu.VMEM((N_HEADS, 1), F32),
                        pltpu.VMEM((N_HEADS, V_DIM), F32)],
    )
    return pl.pallas_call(
        functools.partial(_attention_kernel, lambda_init, cfg),
        out_shape=[jax.ShapeDtypeStruct(qt.shape, BF16), sample_out],
        grid_spec=grid_spec, compiler_params=_params(3), name="attention",
    )(page_table, qt, kb, vt, lamv, subg_tile, qs, kn, vn, subg_row, cache_kt, cache_vf)


class _SampleCfg:
    def __init__(self, layer, nb, n_pages, n_steps):
        assert n_steps % (2 * nb) == 0 and n_pages % (n_steps // nb // 2) == 0
        self.layer, self.nb, self.n_pages, self.n_steps = layer, nb, n_pages, n_steps
        self.steps_per_seq = n_steps // nb
        self.key_steps = self.steps_per_seq // 2
        self.pages_per_step = n_pages // self.key_steps
        assert self.pages_per_step % KEY_PAGE_GROUP == 0 and self.pages_per_step % VALUE_PAGE_GROUP == 0


def _sample_stream(lambda_init, cfg, step, pt_ref, lamv_ref, sample_in, o_ref, scratch):
    q_ref, kn_ref, vn_ref, subg_ref, ck_ref, cv_ref = sample_in
    ring, sems, qcol_ref, s0_ref, s1_ref, self_ref, w_ref, wself_ref, acc_ref = scratch
    sps, kst, npp = cfg.steps_per_seq, cfg.key_steps, cfg.pages_per_step
    past = cfg.n_pages * PAGE_SIZE
    seq_i = step // sps
    ph = step % sps
    slope = jnp.exp2(-(lax.broadcasted_iota(jnp.int32, (N_HEADS, 1), 0) + 1).astype(F32))

    def page_copy(src_ref, st, i):
        page = pt_ref[st // sps, ((st % sps) % kst) * npp + i]
        return pltpu.make_async_copy(src_ref.at[cfg.layer, page], ring.at[i], sems.at[i])

    def start_pages(st, slots, extra=True):
        keys = (st % sps) < kst

        @pl.when(jnp.logical_and(extra, keys))
        def _():
            for i in slots:
                page_copy(ck_ref, st, i).start()

        @pl.when(jnp.logical_and(extra, jnp.logical_not(keys)))
        def _():
            for i in slots:
                page_copy(cv_ref, st, i).start()

    def consume(src_ref, page_fn, group):
        for g0 in range(0, npp, group):
            slots = range(g0, g0 + group)
            for i in slots:
                page_copy(src_ref, step, i).wait()
            for i in slots:
                page_fn(i)
            start_pages(step + 1, slots, step + 1 < cfg.n_steps)

    @pl.when(step == 0)
    def _():
        start_pages(step, range(npp))

    @pl.when(ph == 0)
    def _():
        q = q_ref[pl.ds(seq_i, 1), :] * (HEAD_DIM ** -0.5)
        qcol_ref[...] = jnp.broadcast_to(q, (V_DIM, D_MODEL)).T
        row = lax.broadcasted_iota(jnp.int32, (2 * N_HEADS, D_MODEL), 0)
        col = lax.broadcasted_iota(jnp.int32, (2 * N_HEADS, D_MODEL), 1)
        own = (col // V_DIM == row % N_HEADS) & ((col // HEAD_DIM) % 2 == row // N_HEADS)
        prod = jnp.broadcast_to(q * kn_ref[pl.ds(seq_i, 1), :], (2 * N_HEADS, D_MODEL))
        self_ref[...] = jnp.sum(jnp.where(own, prod, 0.0), axis=-1, keepdims=True)

    @pl.when(ph < kst)
    def _():
        qcol = qcol_ref[...]

        def key_page(i):
            pr = (ring[i] * qcol).reshape(N_HEADS, 2, HEAD_DIM, PAGE_SIZE)
            page = ph * npp + i
            kpos = page * PAGE_SIZE + lax.broadcasted_iota(jnp.int32, (1, PAGE_SIZE), 1)
            bias = slope * (kpos - past).astype(F32)
            off = pl.multiple_of(page * PAGE_SIZE, PAGE_SIZE)
            s0_ref[:, pl.ds(off, PAGE_SIZE)] = jnp.sum(pr[:, 0], axis=1) + bias
            s1_ref[:, pl.ds(off, PAGE_SIZE)] = jnp.sum(pr[:, 1], axis=1) + bias

        consume(ck_ref, key_page, KEY_PAGE_GROUP)

    @pl.when(ph == kst)
    def _():
        lam = _lambda(lamv_ref[...], lambda_init)

        def soft(s_ref, s_self):
            s = s_ref[...]
            m = jnp.maximum(jnp.max(s, axis=-1, keepdims=True), s_self)
            p = jnp.exp(s - m)
            p_self = jnp.exp(s_self - m)
            inv = 1.0 / (jnp.sum(p, axis=-1, keepdims=True) + p_self)
            return p * inv, p_self * inv

        w0, w0s = soft(s0_ref, self_ref[:N_HEADS])
        w1, w1s = soft(s1_ref, self_ref[N_HEADS:])
        w_ref[...] = w0 - lam * w1
        wself_ref[...] = w0s - lam * w1s
        acc_ref[...] = jnp.zeros_like(acc_ref)

    @pl.when(ph >= kst)
    def _():
        lane = lax.broadcasted_iota(jnp.int32, (N_HEADS, PAGE_SIZE), 1)
        keep = (lane % N_HEADS) == lax.broadcasted_iota(jnp.int32, (N_HEADS, PAGE_SIZE), 0)
        per = PAGE_SIZE // N_HEADS
        parts = []

        def value_page(i):
            off = pl.multiple_of(((ph - kst) * npp + i) * PAGE_SIZE, PAGE_SIZE)
            w = w_ref[:, pl.ds(off, PAGE_SIZE)]
            wsel = jnp.concatenate(
                [jnp.where(keep, jnp.take_along_axis(w, b * per + lane // N_HEADS, axis=1), 0.0)
                 for b in range(N_HEADS)], axis=1).astype(BF16)
            parts.append(jnp.dot(wsel, ring[i].astype(BF16), preferred_element_type=F32))

        consume(cv_ref, value_page, VALUE_PAGE_GROUP)
        acc_ref[...] = acc_ref[...] + functools.reduce(lambda a, b: a + b, parts)

    @pl.when(ph == sps - 1)
    def _():
        o = acc_ref[...] + wself_ref[...] * vn_ref[seq_i]
        o_ref[seq_i] = _rms(o, subg_ref[...]) * (1.0 - lambda_init)


def kernel(x_prompt, x_sample, cache_k, cache_v, page_table, norm1_g, w_in, lam_q1, lam_k1, lam_q2, lam_k2,
           subln_g, gm_ln_g, gm_ln_b, gm_ws, gm_b, w_pa, w_pb, w_o, norm2_g, w_up, w_down, final_g):
    batch, seq, _ = x_prompt.shape
    nb = x_sample.shape[0]
    depth = w_in.shape[0]
    n_pool = cache_k.shape[1]
    xp = x_prompt.reshape(batch * seq, D_MODEL)
    xs = x_sample.reshape(nb, D_MODEL)
    ckt = jnp.transpose(cache_k, (0, 1, 3, 4, 5, 2)).reshape(depth, n_pool, D_MODEL, PAGE_SIZE)
    cvf = cache_v.reshape(depth, n_pool, PAGE_SIZE * N_HEADS, V_DIM)
    rowv = lambda a: a.reshape(1, -1)
    gf = rowv(final_g)

    kt_all = v_all = None
    ksm, vsm, gvs = [], [], []
    for l in range(depth):
        lambda_init = 0.8 - 0.6 * math.exp(-0.3 * l)
        last = l == depth - 1
        win = w_in[l].astype(BF16)
        wpa, wpb, wo = w_pa[l].astype(BF16), w_pb[l].astype(BF16), w_o[l].astype(BF16)
        wup, wdn = w_up[l].astype(BF16), w_down[l].astype(BF16)
        g1, g2 = rowv(norm1_g[l]), rowv(norm2_g[l])
        lng, lnb = rowv(gm_ln_g[l]), rowv(gm_ln_b[l])
        lamv = jnp.stack([lam_q1[l], lam_k1[l], lam_q2[l], lam_k2[l]])
        subg = rowv(subln_g[l])

        bs_full = jnp.broadcast_to(gm_b[l][:, :, None], (GM_GROUPS, CHUNK, CHUNK))
        qt, kt_all, kb, v_all, vt, b_in, ga, gb = _inproj_prompt(
            xp, g1, win, lng, lnb, gm_ws[l], bs_full, kt_all, v_all,
            layer=l, depth=depth, batch=batch, seq=seq)
        subg_tile = jnp.broadcast_to(subln_g[l][:, None], (V_DIM, V_DIM))
        ws_row = rowv(jnp.repeat(gm_ws[l][:, 0, 0], CHUNK))
        bs_row = rowv(jnp.repeat(gm_b[l][:, 0], CHUNK))
        qs_, kfs, vfs, b_in_s, ga_s, gb_s, gv = _inproj_decode(xs, g1, win, lng, lnb, ws_row, bs_row)

        a_in, a_in_s = _attention(qt, kb, vt, lamv, subg_tile, qs_, kfs, vfs, subg, ckt, cvf, page_table,
                                  batch=batch, seq=seq, layer=l, lambda_init=lambda_init)
        xp = _output(xp, a_in, b_in, ga, gb, wpa, wpb, wo, g2, wup, wdn, gf, final=last, tm=ROW_TILE)
        xs = _output(xs, a_in_s.reshape(nb, D_MODEL).astype(BF16), b_in_s, ga_s, gb_s, wpa, wpb, wo, g2, wup, wdn,
                     gf, final=last, tm=nb)
        ksm.append(kfs)
        vsm.append(vfs)
        gvs.append(gv)

    y_prompt = xp.reshape(batch, seq, D_MODEL)
    y_sample = xs.reshape(nb, 1, D_MODEL)
    k_prompt = jnp.transpose(kt_all.reshape(depth, batch, N_HEADS, 2, HEAD_DIM, seq), (0, 1, 5, 2, 3, 4))
    k_sample = jnp.stack(ksm).reshape(depth, nb, 1, N_HEADS, 2, HEAD_DIM)
    v_sample = jnp.stack(vsm).reshape(depth, nb, 1, N_HEADS, V_DIM)
    gv_sample = jnp.stack(gvs).reshape(depth, nb, 1, D_MODEL)
    return y_prompt, y_sample, k_prompt, v_all, k_sample, v_sample, gv_sample
```

```python
import functools
import math

import jax
import jax.numpy as jnp
from jax import lax
from jax.experimental import pallas as pl
from jax.experimental.pallas import tpu as pltpu

D_MODEL = 1024
N_HEADS = 8
HEAD_DIM = 64
V_DIM = 128
GM_GROUPS = 8
CHUNK = 128
PAGE_SIZE = 128
EPS = 1e-6

F32 = jnp.float32
BF16 = jnp.bfloat16

ROW_TILE = 256
Q_TILE = 512
K_TILE = 256
HEADS_PER_STEP = 2
AUG = 16
KEY_PAGE_GROUP = 4
VALUE_PAGE_GROUP = 8
LOG2E = math.log2(math.e)
VMEM_LIMIT = 56 * 1024 * 1024


def _const_spec(shape):
    nd = len(shape)
    return pl.BlockSpec(shape, lambda *_: (0,) * nd, pipeline_mode=pl.Buffered(1))


def _params(n_axes):
    return pltpu.CompilerParams(dimension_semantics=("arbitrary",) * n_axes, vmem_limit_bytes=VMEM_LIMIT)


def _gelu_tanh(x):
    c = math.sqrt(2.0 / math.pi)
    return 0.5 * x * (1.0 + jnp.tanh(c * (x + 0.044715 * (x * x * x))))


def _rms(x, g):
    return x * lax.rsqrt(jnp.mean(x * x, axis=-1, keepdims=True) + EPS) * g


def _lambda(lamv, lambda_init):
    s1 = jnp.sum(lamv[0:1] * lamv[1:2], axis=-1, keepdims=True)
    s2 = jnp.sum(lamv[2:3] * lamv[3:4], axis=-1, keepdims=True)
    return jnp.exp(s1) - jnp.exp(s2) + lambda_init


def _inproj_kernel(decode, n_alias, x_ref, g1_ref, win_ref, lng_ref, lnb_ref, ws_ref, bs_ref, *rest):
    outs = rest[n_alias:]
    if decode:
        q_ref, kf_ref, vf_ref, bin_ref, ga_ref, gb_ref, gv_ref = outs
    else:
        qt_ref, kt_ref, kb_ref, vf_ref, vt_ref, bin_ref, ga_ref, gb_ref = outs
    rows = x_ref.shape[0]
    h = _rms(x_ref[...], g1_ref[...]).astype(BF16)

    def proj(c):
        return jnp.dot(h, win_ref[:, c * D_MODEL:(c + 1) * D_MODEL], preferred_element_type=F32)

    q = proj(0)
    k = proj(1)
    v = proj(2)
    if decode:
        q_ref[...] = q
        kf_ref[...] = k
    else:
        qt_ref[0] = (q * (LOG2E * HEAD_DIM ** -0.5)).T.astype(BF16)
        kt_ref[0, 0] = k.T
        kb_ref[...] = k.astype(BF16)
        vt_ref[0] = v.T.astype(BF16)
    for hd in range(N_HEADS):
        if decode:
            vf_ref[:, hd, :] = v[:, hd * V_DIM:(hd + 1) * V_DIM]
        else:
            vf_ref[0, 0, :, hd, :] = v[:, hd * V_DIM:(hd + 1) * V_DIM]

    ga_ref[...] = jax.nn.sigmoid(proj(5)).astype(BF16)
    gb_ref[...] = jax.nn.sigmoid(proj(6)).astype(BF16)

    zv = _gelu_tanh(proj(4))
    xc = zv - jnp.mean(zv, axis=-1, keepdims=True)
    vg = xc * lax.rsqrt(jnp.mean(xc * xc, axis=-1, keepdims=True) + EPS) * lng_ref[...] + lnb_ref[...]
    u = _gelu_tanh(proj(3))

    if decode:
        gv_ref[...] = vg
        bin_ref[...] = (u * (ws_ref[...] * vg + bs_ref[...])).astype(BF16)
    else:
        vgb = vg.astype(BF16)
        tril = (lax.broadcasted_iota(jnp.int32, (CHUNK, CHUNK), 0)
                >= lax.broadcasted_iota(jnp.int32, (CHUNK, CHUNK), 1))
        for g in range(GM_GROUPS):
            wm = jnp.where(tril, ws_ref[g], 0.0).astype(BF16)
            cols = slice(g * CHUNK, (g + 1) * CHUNK)
            for c in range(rows // CHUNK):
                rs = slice(c * CHUNK, (c + 1) * CHUNK)
                s = jnp.dot(wm, vgb[rs, cols], preferred_element_type=F32) + bs_ref[g]
                bin_ref[rs, cols] = (u[rs, cols] * s).astype(BF16)


def _inproj_prompt(x, g1, win, lng, lnb, ws, bs, kt_all, v_all, *, layer, depth, batch, seq):
    n = batch * seq
    tm = ROW_TILE
    nt = seq // tm
    row = jax.ShapeDtypeStruct((n, D_MODEL), BF16)
    fmaj = jax.ShapeDtypeStruct((batch, D_MODEL, seq), BF16)
    out_shape = [fmaj,
                 jax.ShapeDtypeStruct((depth, batch, D_MODEL, seq), F32),
                 row,
                 jax.ShapeDtypeStruct((depth, batch, seq, N_HEADS, V_DIM), F32),
                 fmaj, row, row, row]
    tile = pl.BlockSpec((tm, D_MODEL), lambda b, i: (b * nt + i, 0))
    ftile = pl.BlockSpec((1, D_MODEL, tm), lambda b, i: (b, 0, i))
    out_specs = [ftile,
                 pl.BlockSpec((1, 1, D_MODEL, tm), lambda b, i: (layer, b, 0, i)),
                 tile,
                 pl.BlockSpec((1, 1, tm, N_HEADS, V_DIM), lambda b, i: (layer, b, i, 0, 0)),
                 ftile, tile, tile, tile]
    consts = [g1, win, lng, lnb, ws, bs]
    in_specs = [tile] + [_const_spec(c.shape) for c in consts]
    args = [x] + consts
    aliases = {}
    n_alias = 0
    if kt_all is not None:
        in_specs += [pl.BlockSpec(memory_space=pl.ANY)] * 2
        aliases = {len(args): 1, len(args) + 1: 3}
        args += [kt_all, v_all]
        n_alias = 2
    return pl.pallas_call(
        functools.partial(_inproj_kernel, False, n_alias),
        out_shape=out_shape, grid=(batch, nt), in_specs=in_specs, out_specs=out_specs,
        input_output_aliases=aliases, compiler_params=_params(2), name="inproj_prompt",
    )(*args)


def _inproj_decode(x, g1, win, lng, lnb, ws, bs):
    n = x.shape[0]
    row = lambda dt: jax.ShapeDtypeStruct((n, D_MODEL), dt)
    out_shape = [row(F32), row(F32), jax.ShapeDtypeStruct((n, N_HEADS, V_DIM), F32),
                 row(BF16), row(BF16), row(BF16), row(F32)]
    consts = [g1, win, lng, lnb, ws, bs]
    full = lambda s: pl.BlockSpec(s.shape, lambda i: (0,) * len(s.shape))
    return pl.pallas_call(
        functools.partial(_inproj_kernel, True, 0),
        out_shape=out_shape, grid=(1,),
        in_specs=[full(x)] + [_const_spec(c.shape) for c in consts],
        out_specs=[full(s) for s in out_shape],
        compiler_params=_params(1), name="inproj_decode",
    )(x, *consts)


def _output_kernel(final, a_fmajor, x_ref, a_ref, b_ref, ga_ref, gb_ref, wpa_ref, wpb_ref, wo_ref, g2_ref,
                   wup_ref, wdn_ref, gf_ref, o_ref):
    if a_fmajor:
        a = lax.dot_general(a_ref[0], wpa_ref[...], (((0,), (0,)), ((), ())), preferred_element_type=F32)
    else:
        a = jnp.dot(a_ref[...], wpa_ref[...], preferred_element_type=F32)
    b = jnp.dot(b_ref[...], wpb_ref[...], preferred_element_type=F32)
    mix = ga_ref[...].astype(F32) * a + gb_ref[...].astype(F32) * b
    x = x_ref[...] + jnp.dot(mix.astype(BF16), wo_ref[...], preferred_element_type=F32)
    h2 = _rms(x, g2_ref[...]).astype(BF16)
    hid = jnp.maximum(jnp.dot(h2, wup_ref[...], preferred_element_type=F32), 0.0)
    x = x + jnp.dot((hid * hid).astype(BF16), wdn_ref[...], preferred_element_type=F32)
    o_ref[...] = _rms(x, gf_ref[...]) if final else x


def _output(x, a_in, b_in, ga, gb, wpa, wpb, wo, g2, wup, wdn, gf, *, final, tm):
    n = x.shape[0]
    tile = pl.BlockSpec((tm, D_MODEL), lambda i: (i, 0))
    a_fmajor = a_in.ndim == 3
    if a_fmajor:
        nt = a_in.shape[2] // tm
        a_spec = pl.BlockSpec((1, D_MODEL, tm), lambda i: (i // nt, 0, i % nt))
    else:
        a_spec = tile
    consts = [wpa, wpb, wo, g2, wup, wdn, gf]
    return pl.pallas_call(
        functools.partial(_output_kernel, final, a_fmajor),
        out_shape=jax.ShapeDtypeStruct((n, D_MODEL), F32),
        grid=(n // tm,),
        in_specs=[tile, a_spec, tile, tile, tile] + [_const_spec(c.shape) for c in consts],
        out_specs=tile, compiler_params=_params(1), name="output_block",
    )(x, a_in, b_in, ga, gb, *consts)


def _attention_kernel(lambda_init, sample_cfg, pt_ref, q_ref, k_ref, vt_ref, lamv_ref, subg_ref, *rest):
    n_sample_in = 6
    sample_in = rest[:n_sample_in]
    o_ref, os_ref = rest[n_sample_in:n_sample_in + 2]
    sa_scr, sb_scr, m_scr, l_scr, acc_scr = rest[n_sample_in + 2:n_sample_in + 7]
    sample_scr = rest[n_sample_in + 7:]
    hp = pl.program_id(1)
    qi = pl.program_id(2)
    step = (pl.program_id(0) * pl.num_programs(1) + hp) * pl.num_programs(2) + qi
    tq, tk = Q_TILE, K_TILE
    q0 = qi * tq
    kcol = lax.broadcasted_iota(jnp.int32, (tk, V_DIM), 1)
    ramp = jnp.where(kcol < 2, lax.broadcasted_iota(jnp.int32, (tk, V_DIM), 0), 0).astype(F32).astype(BF16)
    ones = jnp.ones((AUG, tk), BF16)
    feat = lax.broadcasted_iota(jnp.int32, (V_DIM, tq), 0)

    chains = []
    for hh in range(HEADS_PER_STEP):
        hd = hp * HEADS_PER_STEP + hh
        qt = q_ref[0, hh * V_DIM:(hh + 1) * V_DIM, :]
        slope = jnp.exp2(-(hd + 1).astype(F32) * jnp.ones((1, tq), F32)) * LOG2E
        s_hi = slope.astype(BF16).astype(F32)
        s_lo = (slope - s_hi).astype(BF16).astype(F32)
        aug_rows = jnp.where(feat == 0, s_hi, jnp.where(feat == 1, s_lo, 0.0)).astype(BF16)
        for mp in range(2):
            keep = (feat < HEAD_DIM) if mp == 0 else (feat >= HEAD_DIM)
            w = jnp.concatenate([jnp.where(keep, qt, jnp.zeros_like(qt)), aug_rows], axis=0)
            chains.append((hh, w, s_hi[:, :1] + s_lo[:, :1]))

    nch = len(chains)

    all_q = slice(0, tq)
    late_q = slice(tk, tq)

    def scores(j, ci, buf, cols=all_q):
        hh, w, _ = chains[ci]
        k0 = pl.multiple_of(j * tk, tk)
        kaug = jnp.concatenate([k_ref[pl.ds(k0, tk), hh * V_DIM:(hh + 1) * V_DIM], ramp], axis=1)
        buf[ci, :, cols] = jnp.dot(kaug, w[:, cols], preferred_element_type=F32)

    def update(j, ci, buf, masked, cols=all_q):
        hh, _, slope_v = chains[ci]
        s = buf[ci, :, cols]
        nq = cols.stop - cols.start
        k0 = pl.multiple_of(j * tk, tk)
        c = slope_v * (k0 - q0).astype(F32)
        if masked:
            kk = lax.broadcasted_iota(jnp.int32, (tk, nq), 0) + (k0 - q0)
            qq = lax.broadcasted_iota(jnp.int32, (tk, nq), 1) + cols.start
            s = jnp.where(qq >= kk, s, -jnp.inf)
        m = m_scr[ci, :, cols]
        m_new = jnp.maximum(m, jnp.max(s, axis=0, keepdims=True) + c)
        alpha = jnp.exp2(m - m_new)
        m_scr[ci, :, cols] = m_new
        p = jnp.exp2(s - (m_new - c)).astype(BF16)
        vaug = jnp.concatenate([vt_ref[0, hh * V_DIM:(hh + 1) * V_DIM, pl.ds(k0, tk)], ones], axis=0)
        pv = jnp.dot(vaug, p, preferred_element_type=F32)
        l_scr[ci, :, cols] = alpha * l_scr[ci, :, cols] + pv[V_DIM:V_DIM + 8]
        acc_scr[ci, :, cols] = alpha * acc_scr[ci, :, cols] + pv[:V_DIM]

    def stage(j_next, buf_next, j, buf, masked, cols_next=all_q, cols=all_q):
        if j_next is not None:
            for ci in range(nch):
                scores(j_next, ci, buf_next, cols_next)
        for ci in range(nch):
            update(j, ci, buf, masked, cols)

    def body(t, carry):
        stage(2 * t + 1, sb_scr, 2 * t, sa_scr, False)
        stage(2 * t + 2, sa_scr, 2 * t + 1, sb_scr, False)
        return carry

    m_scr[...] = jnp.full(m_scr.shape, -jnp.inf, F32)
    l_scr[...] = jnp.zeros(l_scr.shape, F32)
    acc_scr[...] = jnp.zeros(acc_scr.shape, F32)
    for ci in range(nch):
        scores(0, ci, sa_scr)
    _sample_stream(lambda_init, sample_cfg, step, pt_ref, lamv_ref, sample_in, os_ref, sample_scr)
    lax.fori_loop(0, qi, body, 0)
    stage(2 * qi + 1, sb_scr, 2 * qi, sa_scr, True, cols_next=late_q)
    stage(None, None, 2 * qi + 1, sb_scr, True, cols=late_q)

    lam = _lambda(lamv_ref[...], lambda_init)
    subg = jnp.concatenate([subg_ref[...]] * (tq // V_DIM), axis=1)
    for hh in range(HEADS_PER_STEP):
        o0 = acc_scr[2 * hh] / l_scr[2 * hh, 0:1]
        o1 = acc_scr[2 * hh + 1] / l_scr[2 * hh + 1, 0:1]
        ot = o0 - lam * o1
        ot = ot * lax.rsqrt(jnp.mean(ot * ot, axis=0, keepdims=True) + EPS) * subg
        o_ref[0, hh * V_DIM:(hh + 1) * V_DIM, :] = (ot * (1.0 - lambda_init)).astype(BF16)


def _attention(qt, kb, vt, lamv, subg_tile, qs, kn, vn, subg_row, cache_kt, cache_vf, page_table, *,
               batch, seq, layer, lambda_init):
    assert Q_TILE == 2 * K_TILE and seq % Q_TILE == 0
    hps = HEADS_PER_STEP
    grid = (batch, N_HEADS // hps, seq // Q_TILE)
    n_steps = grid[0] * grid[1] * grid[2]
    nb, n_pages = page_table.shape
    cfg = _SampleCfg(layer, nb, n_pages, n_steps)
    past = n_pages * PAGE_SIZE
    fspec = pl.BlockSpec((1, hps * V_DIM, Q_TILE), lambda b, h, i, pt: (b, h, i))
    whole = lambda a: pl.BlockSpec(a.shape, lambda b, h, i, pt: (0,) * a.ndim, pipeline_mode=pl.Buffered(1))
    hbm = pl.BlockSpec(memory_space=pl.ANY)
    nch = 2 * hps
    sample_out = jax.ShapeDtypeStruct((nb, N_HEADS, V_DIM), F32)
    grid_spec = pltpu.PrefetchScalarGridSpec(
        num_scalar_prefetch=1, grid=grid,
        in_specs=[fspec,
                  pl.BlockSpec((seq, hps * V_DIM), lambda b, h, i, pt: (b, h)),
                  pl.BlockSpec((1, hps * V_DIM, seq), lambda b, h, i, pt: (b, h, 0)),
                  whole(lamv), whole(subg_tile),
                  whole(qs), whole(kn), whole(vn), whole(subg_row), hbm, hbm],
        out_specs=[fspec, pl.BlockSpec(sample_out.shape, lambda b, h, i, pt: (0, 0, 0))],
        scratch_shapes=[pltpu.VMEM((nch, K_TILE, Q_TILE), F32), pltpu.VMEM((nch, K_TILE, Q_TILE), F32),
                        pltpu.VMEM((nch, 1, Q_TILE), F32), pltpu.VMEM((nch, 8, Q_TILE), F32),
                        pltpu.VMEM((nch, V_DIM, Q_TILE), F32),
                        pltpu.VMEM((cfg.pages_per_step, D_MODEL, PAGE_SIZE), F32),
                        pltpu.SemaphoreType.DMA((cfg.pages_per_step,)),
                        pltpu.VMEM((D_MODEL, PAGE_SIZE), F32),
                        pltpu.VMEM((N_HEADS, past), F32),
                        pltpu.VMEM((N_HEADS, past), F32),
                        pltpu.VMEM((2 * N_HEADS, 1), F32),
                        pltpu.VMEM((N_HEADS, past), F32),
                        pltpu.VMEM((N_HEADS, 1), F32),
                        pltpu.VMEM((N_HEADS, V_DIM), F32)],
    )
    return pl.pallas_call(
        functools.partial(_attention_kernel, lambda_init, cfg),
        out_shape=[jax.ShapeDtypeStruct(qt.shape, BF16), sample_out],
        grid_spec=grid_spec, compiler_params=_params(3), name="attention",
    )(page_table, qt, kb, vt, lamv, subg_tile, qs, kn, vn, subg_row, cache_kt, cache_vf)


class _SampleCfg:
    def __init__(self, layer, nb, n_pages, n_steps):
        assert n_steps % (2 * nb) == 0 and n_pages % (n_steps // nb // 2) == 0
        self.layer, self.nb, self.n_pages, self.n_steps = layer, nb, n_pages, n_steps
        self.steps_per_seq = n_steps // nb
        self.key_steps = self.steps_per_seq // 2
        self.pages_per_step = n_pages // self.key_steps
        assert self.pages_per_step % KEY_PAGE_GROUP == 0 and self.pages_per_step % VALUE_PAGE_GROUP == 0


def _sample_stream(lambda_init, cfg, step, pt_ref, lamv_ref, sample_in, o_ref, scratch):
    q_ref, kn_ref, vn_ref, subg_ref, ck_ref, cv_ref = sample_in
    ring, sems, qcol_ref, s0_ref, s1_ref, self_ref, w_ref, wself_ref, acc_ref = scratch
    sps, kst, npp = cfg.steps_per_seq, cfg.key_steps, cfg.pages_per_step
    past = cfg.n_pages * PAGE_SIZE
    seq_i = step // sps
    ph = step % sps

    def page_copy(src_ref, st, i):
        page = pt_ref[st // sps, ((st % sps) % kst) * npp + i]
        return pltpu.make_async_copy(src_ref.at[cfg.layer, page], ring.at[i], sems.at[i])

    def start_pages(st, slots, extra=True):
        keys = (st % sps) < kst

        @pl.when(jnp.logical_and(extra, keys))
        def _():
            for i in slots:
                page_copy(ck_ref, st, i).start()

        @pl.when(jnp.logical_and(extra, jnp.logical_not(keys)))
        def _():
            for i in slots:
                page_copy(cv_ref, st, i).start()

    def consume(src_ref, group_fn, group):
        for g0 in range(0, npp, group):
            slots = range(g0, g0 + group)
            for i in slots:
                page_copy(src_ref, step, i).wait()
            group_fn(slots)
            start_pages(step + 1, slots, step + 1 < cfg.n_steps)

    @pl.when(step == 0)
    def _():
        start_pages(step, range(npp))

    @pl.when(ph == 0)
    def _():
        q = q_ref[pl.ds(seq_i, 1), :] * (HEAD_DIM ** -0.5)
        qcol_ref[...] = jnp.broadcast_to(q, (V_DIM, D_MODEL)).T
        row = lax.broadcasted_iota(jnp.int32, (2 * N_HEADS, D_MODEL), 0)
        col = lax.broadcasted_iota(jnp.int32, (2 * N_HEADS, D_MODEL), 1)
        own = (col // V_DIM == row % N_HEADS) & ((col // HEAD_DIM) % 2 == row // N_HEADS)
        prod = jnp.broadcast_to(q * kn_ref[pl.ds(seq_i, 1), :], (2 * N_HEADS, D_MODEL))
        self_ref[...] = jnp.sum(jnp.where(own, prod, 0.0), axis=-1, keepdims=True)

    @pl.when(ph < kst)
    def _():
        def key_pages(slots):
            dist = [((ph * npp + i) * PAGE_SIZE - past
                     + lax.broadcasted_iota(jnp.int32, (1, PAGE_SIZE), 1)).astype(F32) for i in slots]
            for seg in range(2 * N_HEADS):
                hd, mp = divmod(seg, 2)
                rows = slice(seg * HEAD_DIM, (seg + 1) * HEAD_DIM)
                qc = qcol_ref[rows, :]
                s_ref = s1_ref if mp else s0_ref
                for n, i in enumerate(slots):
                    sc = jnp.sum(ring[i, rows, :] * qc, axis=0, keepdims=True)
                    off = pl.multiple_of((ph * npp + i) * PAGE_SIZE, PAGE_SIZE)
                    s_ref[hd:hd + 1, pl.ds(off, PAGE_SIZE)] = sc + 2.0 ** -(hd + 1) * dist[n]

        consume(ck_ref, key_pages, KEY_PAGE_GROUP)

    @pl.when(ph == kst)
    def _():
        lam = _lambda(lamv_ref[...], lambda_init)

        def soft(s_ref, s_self):
            s = s_ref[...]
            m = jnp.maximum(jnp.max(s, axis=-1, keepdims=True), s_self)
            p = jnp.exp(s - m)
            p_self = jnp.exp(s_self - m)
            inv = 1.0 / (jnp.sum(p, axis=-1, keepdims=True) + p_self)
            return p * inv, p_self * inv

        w0, w0s = soft(s0_ref, self_ref[:N_HEADS])
        w1, w1s = soft(s1_ref, self_ref[N_HEADS:])
        w_ref[...] = w0 - lam * w1
        wself_ref[...] = w0s - lam * w1s
        acc_ref[...] = jnp.zeros_like(acc_ref)

    @pl.when(ph >= kst)
    def _():
        lane = lax.broadcasted_iota(jnp.int32, (N_HEADS, PAGE_SIZE), 1)
        keep = (lane % N_HEADS) == lax.broadcasted_iota(jnp.int32, (N_HEADS, PAGE_SIZE), 0)
        per = PAGE_SIZE // N_HEADS
        parts = []

        def value_pages(slots):
            for i in slots:
                off = pl.multiple_of(((ph - kst) * npp + i) * PAGE_SIZE, PAGE_SIZE)
                w = w_ref[:, pl.ds(off, PAGE_SIZE)]
                wsel = jnp.concatenate(
                    [jnp.where(keep, jnp.take_along_axis(w, b * per + lane // N_HEADS, axis=1), 0.0)
                     for b in range(N_HEADS)], axis=1).astype(BF16)
                parts.append(jnp.dot(wsel, ring[i].astype(BF16), preferred_element_type=F32))

        consume(cv_ref, value_pages, VALUE_PAGE_GROUP)
        acc_ref[...] = acc_ref[...] + functools.reduce(lambda a, b: a + b, parts)

    @pl.when(ph == sps - 1)
    def _():
        o = acc_ref[...] + wself_ref[...] * vn_ref[seq_i]
        o_ref[seq_i] = _rms(o, subg_ref[...]) * (1.0 - lambda_init)


def kernel(x_prompt, x_sample, cache_k, cache_v, page_table, norm1_g, w_in, lam_q1, lam_k1, lam_q2, lam_k2,
           subln_g, gm_ln_g, gm_ln_b, gm_ws, gm_b, w_pa, w_pb, w_o, norm2_g, w_up, w_down, final_g):
    batch, seq, _ = x_prompt.shape
    nb = x_sample.shape[0]
    depth = w_in.shape[0]
    n_pool = cache_k.shape[1]
    xp = x_prompt.reshape(batch * seq, D_MODEL)
    xs = x_sample.reshape(nb, D_MODEL)
    ckt = jnp.transpose(cache_k, (0, 1, 3, 4, 5, 2)).reshape(depth, n_pool, D_MODEL, PAGE_SIZE)
    cvf = cache_v.reshape(depth, n_pool, PAGE_SIZE * N_HEADS, V_DIM)
    rowv = lambda a: a.reshape(1, -1)
    gf = rowv(final_g)

    kt_all = v_all = None
    ksm, vsm, gvs = [], [], []
    for l in range(depth):
        lambda_init = 0.8 - 0.6 * math.exp(-0.3 * l)
        last = l == depth - 1
        win = w_in[l].astype(BF16)
        wpa, wpb, wo = w_pa[l].astype(BF16), w_pb[l].astype(BF16), w_o[l].astype(BF16)
        wup, wdn = w_up[l].astype(BF16), w_down[l].astype(BF16)
        g1, g2 = rowv(norm1_g[l]), rowv(norm2_g[l])
        lng, lnb = rowv(gm_ln_g[l]), rowv(gm_ln_b[l])
        lamv = jnp.stack([lam_q1[l], lam_k1[l], lam_q2[l], lam_k2[l]])
        subg = rowv(subln_g[l])

        bs_full = jnp.broadcast_to(gm_b[l][:, :, None], (GM_GROUPS, CHUNK, CHUNK))
        qt, kt_all, kb, v_all, vt, b_in, ga, gb = _inproj_prompt(
            xp, g1, win, lng, lnb, gm_ws[l], bs_full, kt_all, v_all,
            layer=l, depth=depth, batch=batch, seq=seq)
        subg_tile = jnp.broadcast_to(subln_g[l][:, None], (V_DIM, V_DIM))
        ws_row = rowv(jnp.repeat(gm_ws[l][:, 0, 0], CHUNK))
        bs_row = rowv(jnp.repeat(gm_b[l][:, 0], CHUNK))
        qs_, kfs, vfs, b_in_s, ga_s, gb_s, gv = _inproj_decode(xs, g1, win, lng, lnb, ws_row, bs_row)

        a_in, a_in_s = _attention(qt, kb, vt, lamv, subg_tile, qs_, kfs, vfs, subg, ckt, cvf, page_table,
                                  batch=batch, seq=seq, layer=l, lambda_init=lambda_init)
        xp = _output(xp, a_in, b_in, ga, gb, wpa, wpb, wo, g2, wup, wdn, gf, final=last, tm=ROW_TILE)
        xs = _output(xs, a_in_s.reshape(nb, D_MODEL).astype(BF16), b_in_s, ga_s, gb_s, wpa, wpb, wo, g2, wup, wdn,
                     gf, final=last, tm=nb)
        ksm.append(kfs)
        vsm.append(vfs)
        gvs.append(gv)

    y_prompt = xp.reshape(batch, seq, D_MODEL)
    y_sample = xs.reshape(nb, 1, D_MODEL)
    k_prompt = jnp.transpose(kt_all.reshape(depth, batch, N_HEADS, 2, HEAD_DIM, seq), (0, 1, 5, 2, 3, 4))
    k_sample = jnp.stack(ksm).reshape(depth, nb, 1, N_HEADS, 2, HEAD_DIM)
    v_sample = jnp.stack(vsm).reshape(depth, nb, 1, N_HEADS, V_DIM)
    gv_sample = jnp.stack(gvs).reshape(depth, nb, 1, D_MODEL)
    return y_prompt, y_sample, k_prompt, v_all, k_sample, v_sample, gv_sample
```

```python
import functools
import math

import jax
import jax.numpy as jnp
from jax import lax
from jax.experimental import pallas as pl
from jax.experimental.pallas import tpu as pltpu

D_MODEL = 1024
N_HEADS = 8
HEAD_DIM = 64
V_DIM = 128
GM_GROUPS = 8
CHUNK = 128
PAGE_SIZE = 128
EPS = 1e-6

F32 = jnp.float32
BF16 = jnp.bfloat16

ROW_TILE = 256
Q_TILE = 512
K_TILE = 256
HEADS_PER_STEP = 2
AUG = 16
KEY_PAGE_GROUP = 4
VALUE_PAGE_GROUP = 8
LOG2E = math.log2(math.e)
VMEM_LIMIT = 56 * 1024 * 1024


def _const_spec(shape):
    nd = len(shape)
    return pl.BlockSpec(shape, lambda *_: (0,) * nd, pipeline_mode=pl.Buffered(1))


def _params(n_axes):
    return pltpu.CompilerParams(dimension_semantics=("arbitrary",) * n_axes, vmem_limit_bytes=VMEM_LIMIT)


def _gelu_tanh(x):
    c = math.sqrt(2.0 / math.pi)
    return 0.5 * x * (1.0 + jnp.tanh(c * (x + 0.044715 * (x * x * x))))


def _rms(x, g):
    return x * lax.rsqrt(jnp.mean(x * x, axis=-1, keepdims=True) + EPS) * g


def _lambda(lamv, lambda_init):
    s1 = jnp.sum(lamv[0:1] * lamv[1:2], axis=-1, keepdims=True)
    s2 = jnp.sum(lamv[2:3] * lamv[3:4], axis=-1, keepdims=True)
    return jnp.exp(s1) - jnp.exp(s2) + lambda_init


def _inproj_kernel(decode, n_alias, x_ref, g1_ref, win_ref, lng_ref, lnb_ref, ws_ref, bs_ref, *rest):
    outs = rest[n_alias:]
    if decode:
        q_ref, kf_ref, vf_ref, bin_ref, ga_ref, gb_ref, gv_ref = outs
    else:
        qt_ref, kt_ref, kb_ref, vf_ref, vt_ref, bin_ref, ga_ref, gb_ref = outs
    rows = x_ref.shape[0]
    h = _rms(x_ref[...], g1_ref[...]).astype(BF16)

    def proj(c):
        return jnp.dot(h, win_ref[:, c * D_MODEL:(c + 1) * D_MODEL], preferred_element_type=F32)

    zv = _gelu_tanh(proj(4))
    xc = zv - jnp.mean(zv, axis=-1, keepdims=True)
    vg = xc * lax.rsqrt(jnp.mean(xc * xc, axis=-1, keepdims=True) + EPS) * lng_ref[...] + lnb_ref[...]
    u = _gelu_tanh(proj(3))

    q = proj(0)
    k = proj(1)
    v = proj(2)
    if decode:
        q_ref[...] = q
        kf_ref[...] = k
    else:
        qt_ref[0] = (q * (LOG2E * HEAD_DIM ** -0.5)).T.astype(BF16)
        kt_ref[0, 0] = k.T
        kb_ref[...] = k.astype(BF16)
        vt_ref[0] = v.T.astype(BF16)
    v_tiles = pltpu.einshape("t(hd)->thd", v, h=N_HEADS)
    if decode:
        vf_ref[...] = v_tiles
    else:
        vf_ref[0, 0] = v_tiles

    if decode:
        gv_ref[...] = vg
        bin_ref[...] = (u * (ws_ref[...] * vg + bs_ref[...])).astype(BF16)
    else:
        vgb = vg.astype(BF16)
        tril = (lax.broadcasted_iota(jnp.int32, (CHUNK, CHUNK), 0)
                >= lax.broadcasted_iota(jnp.int32, (CHUNK, CHUNK), 1))
        for g in range(GM_GROUPS):
            wm = jnp.where(tril, ws_ref[g], 0.0).astype(BF16)
            cols = slice(g * CHUNK, (g + 1) * CHUNK)
            for c in range(rows // CHUNK):
                rs = slice(c * CHUNK, (c + 1) * CHUNK)
                s = jnp.dot(wm, vgb[rs, cols], preferred_element_type=F32) + bs_ref[g]
                bin_ref[rs, cols] = (u[rs, cols] * s).astype(BF16)

    ga_ref[...] = jax.nn.sigmoid(proj(5)).astype(BF16)
    gb_ref[...] = jax.nn.sigmoid(proj(6)).astype(BF16)


def _inproj_prompt(x, g1, win, lng, lnb, ws, bs, kt_all, v_all, *, layer, depth, batch, seq):
    n = batch * seq
    tm = ROW_TILE
    nt = seq // tm
    row = jax.ShapeDtypeStruct((n, D_MODEL), BF16)
    fmaj = jax.ShapeDtypeStruct((batch, D_MODEL, seq), BF16)
    out_shape = [fmaj,
                 jax.ShapeDtypeStruct((depth, batch, D_MODEL, seq), F32),
                 row,
                 jax.ShapeDtypeStruct((depth, batch, seq, N_HEADS, V_DIM), F32),
                 fmaj, row, row, row]
    tile = pl.BlockSpec((tm, D_MODEL), lambda b, i: (b * nt + i, 0))
    ftile = pl.BlockSpec((1, D_MODEL, tm), lambda b, i: (b, 0, i))
    out_specs = [ftile,
                 pl.BlockSpec((1, 1, D_MODEL, tm), lambda b, i: (layer, b, 0, i)),
                 tile,
                 pl.BlockSpec((1, 1, tm, N_HEADS, V_DIM), lambda b, i: (layer, b, i, 0, 0)),
                 ftile, tile, tile, tile]
    consts = [g1, win, lng, lnb, ws, bs]
    in_specs = [tile] + [_const_spec(c.shape) for c in consts]
    args = [x] + consts
    aliases = {}
    n_alias = 0
    if kt_all is not None:
        in_specs += [pl.BlockSpec(memory_space=pl.ANY)] * 2
        aliases = {len(args): 1, len(args) + 1: 3}
        args += [kt_all, v_all]
        n_alias = 2
    return pl.pallas_call(
        functools.partial(_inproj_kernel, False, n_alias),
        out_shape=out_shape, grid=(batch, nt), in_specs=in_specs, out_specs=out_specs,
        input_output_aliases=aliases, compiler_params=_params(2), name="inproj_prompt",
    )(*args)


def _inproj_decode(x, g1, win, lng, lnb, ws, bs):
    n = x.shape[0]
    row = lambda dt: jax.ShapeDtypeStruct((n, D_MODEL), dt)
    out_shape = [row(F32), row(F32), jax.ShapeDtypeStruct((n, N_HEADS, V_DIM), F32),
                 row(BF16), row(BF16), row(BF16), row(F32)]
    consts = [g1, win, lng, lnb, ws, bs]
    full = lambda s: pl.BlockSpec(s.shape, lambda i: (0,) * len(s.shape))
    return pl.pallas_call(
        functools.partial(_inproj_kernel, True, 0),
        out_shape=out_shape, grid=(1,),
        in_specs=[full(x)] + [_const_spec(c.shape) for c in consts],
        out_specs=[full(s) for s in out_shape],
        compiler_params=_params(1), name="inproj_decode",
    )(x, *consts)


def _output_kernel(final, a_fmajor, x_ref, a_ref, b_ref, ga_ref, gb_ref, wpa_ref, wpb_ref, wo_ref, g2_ref,
                   wup_ref, wdn_ref, gf_ref, o_ref):
    if a_fmajor:
        a = lax.dot_general(a_ref[0], wpa_ref[...], (((0,), (0,)), ((), ())), preferred_element_type=F32)
    else:
        a = jnp.dot(a_ref[...], wpa_ref[...], preferred_element_type=F32)
    b = jnp.dot(b_ref[...], wpb_ref[...], preferred_element_type=F32)
    mix = ga_ref[...].astype(F32) * a + gb_ref[...].astype(F32) * b
    x = x_ref[...] + jnp.dot(mix.astype(BF16), wo_ref[...], preferred_element_type=F32)
    h2 = _rms(x, g2_ref[...]).astype(BF16)
    hid = jnp.maximum(jnp.dot(h2, wup_ref[...], preferred_element_type=F32), 0.0)
    x = x + jnp.dot((hid * hid).astype(BF16), wdn_ref[...], preferred_element_type=F32)
    o_ref[...] = _rms(x, gf_ref[...]) if final else x


def _output(x, a_in, b_in, ga, gb, wpa, wpb, wo, g2, wup, wdn, gf, *, final, tm):
    n = x.shape[0]
    tile = pl.BlockSpec((tm, D_MODEL), lambda i: (i, 0))
    a_fmajor = a_in.ndim == 3
    if a_fmajor:
        nt = a_in.shape[2] // tm
        a_spec = pl.BlockSpec((1, D_MODEL, tm), lambda i: (i // nt, 0, i % nt))
    else:
        a_spec = tile
    consts = [wpa, wpb, wo, g2, wup, wdn, gf]
    return pl.pallas_call(
        functools.partial(_output_kernel, final, a_fmajor),
        out_shape=jax.ShapeDtypeStruct((n, D_MODEL), F32),
        grid=(n // tm,),
        in_specs=[tile, a_spec, tile, tile, tile] + [_const_spec(c.shape) for c in consts],
        out_specs=tile, compiler_params=_params(1), name="output_block",
    )(x, a_in, b_in, ga, gb, *consts)


def _attention_kernel(lambda_init, sample_cfg, pt_ref, q_ref, k_ref, vt_ref, lamv_ref, subg_ref, *rest):
    n_sample_in = 6
    sample_in = rest[:n_sample_in]
    o_ref, os_ref = rest[n_sample_in:n_sample_in + 2]
    sa_scr, sb_scr, m_scr, l_scr, acc_scr = rest[n_sample_in + 2:n_sample_in + 7]
    sample_scr = rest[n_sample_in + 7:]
    hp = pl.program_id(1)
    qi = pl.program_id(2)
    step = (pl.program_id(0) * pl.num_programs(1) + hp) * pl.num_programs(2) + qi
    tq, tk = Q_TILE, K_TILE
    q0 = qi * tq
    kcol = lax.broadcasted_iota(jnp.int32, (tk, V_DIM), 1)
    ramp = jnp.where(kcol < 2, lax.broadcasted_iota(jnp.int32, (tk, V_DIM), 0), 0).astype(F32).astype(BF16)
    ones = jnp.ones((AUG, tk), BF16)
    feat = lax.broadcasted_iota(jnp.int32, (V_DIM, tq), 0)

    chains = []
    for hh in range(HEADS_PER_STEP):
        hd = hp * HEADS_PER_STEP + hh
        qt = q_ref[0, hh * V_DIM:(hh + 1) * V_DIM, :]
        slope = jnp.exp2(-(hd + 1).astype(F32) * jnp.ones((1, tq), F32)) * LOG2E
        s_hi = slope.astype(BF16).astype(F32)
        s_lo = (slope - s_hi).astype(BF16).astype(F32)
        aug_rows = jnp.where(feat == 0, s_hi, jnp.where(feat == 1, s_lo, 0.0)).astype(BF16)
        for mp in range(2):
            keep = (feat < HEAD_DIM) if mp == 0 else (feat >= HEAD_DIM)
            w = jnp.concatenate([jnp.where(keep, qt, jnp.zeros_like(qt)), aug_rows], axis=0)
            chains.append((hh, w, s_hi[:, :1] + s_lo[:, :1]))

    nch = len(chains)

    all_q = slice(0, tq)
    late_q = slice(tk, tq)

    def scores(j, ci, buf, cols=all_q):
        hh, w, _ = chains[ci]
        k0 = pl.multiple_of(j * tk, tk)
        kaug = jnp.concatenate([k_ref[pl.ds(k0, tk), hh * V_DIM:(hh + 1) * V_DIM], ramp], axis=1)
        buf[ci, :, cols] = jnp.dot(kaug, w[:, cols], preferred_element_type=F32)

    def update(j, ci, buf, masked, cols=all_q):
        hh, _, slope_v = chains[ci]
        s = buf[ci, :, cols]
        nq = cols.stop - cols.start
        k0 = pl.multiple_of(j * tk, tk)
        c = slope_v * (k0 - q0).astype(F32)
        if masked:
            kk = lax.broadcasted_iota(jnp.int32, (tk, nq), 0) + (k0 - q0)
            qq = lax.broadcasted_iota(jnp.int32, (tk, nq), 1) + cols.start
            s = jnp.where(qq >= kk, s, -jnp.inf)
        m = m_scr[ci, :, cols]
        m_new = jnp.maximum(m, jnp.max(s, axis=0, keepdims=True) + c)
        alpha = jnp.exp2(m - m_new)
        m_scr[ci, :, cols] = m_new
        p = jnp.exp2(s - (m_new - c)).astype(BF16)
        vaug = jnp.concatenate([vt_ref[0, hh * V_DIM:(hh + 1) * V_DIM, pl.ds(k0, tk)], ones], axis=0)
        pv = jnp.dot(vaug, p, preferred_element_type=F32)
        l_scr[ci, :, cols] = alpha * l_scr[ci, :, cols] + pv[V_DIM:V_DIM + 8]
        acc_scr[ci, :, cols] = alpha * acc_scr[ci, :, cols] + pv[:V_DIM]

    def stage(j_next, buf_next, j, buf, masked, cols_next=all_q, cols=all_q):
        if j_next is not None:
            for ci in range(nch):
                scores(j_next, ci, buf_next, cols_next)
        for ci in range(nch):
            update(j, ci, buf, masked, cols)

    def body(t, carry):
        stage(2 * t + 1, sb_scr, 2 * t, sa_scr, False)
        stage(2 * t + 2, sa_scr, 2 * t + 1, sb_scr, False)
        return carry

    m_scr[...] = jnp.full(m_scr.shape, -jnp.inf, F32)
    l_scr[...] = jnp.zeros(l_scr.shape, F32)
    acc_scr[...] = jnp.zeros(acc_scr.shape, F32)
    for ci in range(nch):
        scores(0, ci, sa_scr)
    _sample_stream(lambda_init, sample_cfg, step, pt_ref, lamv_ref, sample_in, os_ref, sample_scr)
    lax.fori_loop(0, qi, body, 0)
    stage(2 * qi + 1, sb_scr, 2 * qi, sa_scr, True, cols_next=late_q)
    stage(None, None, 2 * qi + 1, sb_scr, True, cols=late_q)

    lam = _lambda(lamv_ref[...], lambda_init)
    subg = jnp.concatenate([subg_ref[...]] * (tq // V_DIM), axis=1)
    for hh in range(HEADS_PER_STEP):
        o0 = acc_scr[2 * hh] / l_scr[2 * hh, 0:1]
        o1 = acc_scr[2 * hh + 1] / l_scr[2 * hh + 1, 0:1]
        ot = o0 - lam * o1
        ot = ot * lax.rsqrt(jnp.mean(ot * ot, axis=0, keepdims=True) + EPS) * subg
        o_ref[0, hh * V_DIM:(hh + 1) * V_DIM, :] = (ot * (1.0 - lambda_init)).astype(BF16)


def _attention(qt, kb, vt, lamv, subg_tile, qs, kn, vn, subg_row, cache_kt, cache_vf, page_table, *,
               batch, seq, layer, lambda_init):
    assert Q_TILE == 2 * K_TILE and seq % Q_TILE == 0
    hps = HEADS_PER_STEP
    grid = (batch, N_HEADS // hps, seq // Q_TILE)
    n_steps = grid[0] * grid[1] * grid[2]
    nb, n_pages = page_table.shape
    cfg = _SampleCfg(layer, nb, n_pages, n_steps)
    past = n_pages * PAGE_SIZE
    fspec = pl.BlockSpec((1, hps * V_DIM, Q_TILE), lambda b, h, i, pt: (b, h, i))
    whole = lambda a: pl.BlockSpec(a.shape, lambda b, h, i, pt: (0,) * a.ndim, pipeline_mode=pl.Buffered(1))
    hbm = pl.BlockSpec(memory_space=pl.ANY)
    nch = 2 * hps
    sample_out = jax.ShapeDtypeStruct((nb, N_HEADS, V_DIM), F32)
    grid_spec = pltpu.PrefetchScalarGridSpec(
        num_scalar_prefetch=1, grid=grid,
        in_specs=[fspec,
                  pl.BlockSpec((seq, hps * V_DIM), lambda b, h, i, pt: (b, h)),
                  pl.BlockSpec((1, hps * V_DIM, seq), lambda b, h, i, pt: (b, h, 0)),
                  whole(lamv), whole(subg_tile),
                  whole(qs), whole(kn), whole(vn), whole(subg_row), hbm, hbm],
        out_specs=[fspec, pl.BlockSpec(sample_out.shape, lambda b, h, i, pt: (0, 0, 0))],
        scratch_shapes=[pltpu.VMEM((nch, K_TILE, Q_TILE), F32), pltpu.VMEM((nch, K_TILE, Q_TILE), F32),
                        pltpu.VMEM((nch, 1, Q_TILE), F32), pltpu.VMEM((nch, 8, Q_TILE), F32),
                        pltpu.VMEM((nch, V_DIM, Q_TILE), F32),
                        pltpu.VMEM((cfg.pages_per_step, D_MODEL, PAGE_SIZE), F32),
                        pltpu.SemaphoreType.DMA((cfg.pages_per_step,)),
                        pltpu.VMEM((D_MODEL, PAGE_SIZE), F32),
                        pltpu.VMEM((N_HEADS, past), F32),
                        pltpu.VMEM((N_HEADS, past), F32),
                        pltpu.VMEM((2 * N_HEADS, 1), F32),
                        pltpu.VMEM((N_HEADS, past), F32),
                        pltpu.VMEM((N_HEADS, 1), F32),
                        pltpu.VMEM((N_HEADS, V_DIM), F32)],
    )
    return pl.pallas_call(
        functools.partial(_attention_kernel, lambda_init, cfg),
        out_shape=[jax.ShapeDtypeStruct(qt.shape, BF16), sample_out],
        grid_spec=grid_spec, compiler_params=_params(3), name="attention",
    )(page_table, qt, kb, vt, lamv, subg_tile, qs, kn, vn, subg_row, cache_kt, cache_vf)


class _SampleCfg:
    def __init__(self, layer, nb, n_pages, n_steps):
        assert n_steps % (2 * nb) == 0 and n_pages % (n_steps // nb // 2) == 0
        self.layer, self.nb, self.n_pages, self.n_steps = layer, nb, n_pages, n_steps
        self.steps_per_seq = n_steps // nb
        self.key_steps = self.steps_per_seq // 2
        self.pages_per_step = n_pages // self.key_steps
        assert self.pages_per_step % KEY_PAGE_GROUP == 0 and self.pages_per_step % VALUE_PAGE_GROUP == 0


def _sample_stream(lambda_init, cfg, step, pt_ref, lamv_ref, sample_in, o_ref, scratch):
    q_ref, kn_ref, vn_ref, subg_ref, ck_ref, cv_ref = sample_in
    ring, sems, qcol_ref, s0_ref, s1_ref, self_ref, w_ref, wself_ref, acc_ref = scratch
    sps, kst, npp = cfg.steps_per_seq, cfg.key_steps, cfg.pages_per_step
    past = cfg.n_pages * PAGE_SIZE
    seq_i = step // sps
    ph = step % sps

    def page_copy(src_ref, st, i):
        page = pt_ref[st // sps, ((st % sps) % kst) * npp + i]
        return pltpu.make_async_copy(src_ref.at[cfg.layer, page], ring.at[i], sems.at[i])

    def start_pages(st, slots, extra=True):
        keys = (st % sps) < kst

        @pl.when(jnp.logical_and(extra, keys))
        def _():
            for i in slots:
                page_copy(ck_ref, st, i).start()

        @pl.when(jnp.logical_and(extra, jnp.logical_not(keys)))
        def _():
            for i in slots:
                page_copy(cv_ref, st, i).start()

    def consume(src_ref, group_fn, group):
        for g0 in range(0, npp, group):
            slots = range(g0, g0 + group)
            for i in slots:
                page_copy(src_ref, step, i).wait()
            group_fn(slots)
            start_pages(step + 1, slots, step + 1 < cfg.n_steps)

    @pl.when(step == 0)
    def _():
        start_pages(step, range(npp))

    @pl.when(ph == 0)
    def _():
        q = q_ref[pl.ds(seq_i, 1), :] * (HEAD_DIM ** -0.5)
        qcol_ref[...] = jnp.broadcast_to(q, (V_DIM, D_MODEL)).T
        row = lax.broadcasted_iota(jnp.int32, (2 * N_HEADS, D_MODEL), 0)
        col = lax.broadcasted_iota(jnp.int32, (2 * N_HEADS, D_MODEL), 1)
        own = (col // V_DIM == row % N_HEADS) & ((col // HEAD_DIM) % 2 == row // N_HEADS)
        prod = jnp.broadcast_to(q * kn_ref[pl.ds(seq_i, 1), :], (2 * N_HEADS, D_MODEL))
        self_ref[...] = jnp.sum(jnp.where(own, prod, 0.0), axis=-1, keepdims=True)

    @pl.when(ph < kst)
    def _():
        def key_pages(slots):
            dist = [((ph * npp + i) * PAGE_SIZE - past
                     + lax.broadcasted_iota(jnp.int32, (1, PAGE_SIZE), 1)).astype(F32) for i in slots]
            for seg in range(2 * N_HEADS):
                hd, mp = divmod(seg, 2)
                rows = slice(seg * HEAD_DIM, (seg + 1) * HEAD_DIM)
                qc = qcol_ref[rows, :]
                s_ref = s1_ref if mp else s0_ref
                for n, i in enumerate(slots):
                    sc = jnp.sum(ring[i, rows, :] * qc, axis=0, keepdims=True)
                    off = pl.multiple_of((ph * npp + i) * PAGE_SIZE, PAGE_SIZE)
                    s_ref[hd:hd + 1, pl.ds(off, PAGE_SIZE)] = sc + 2.0 ** -(hd + 1) * dist[n]

        consume(ck_ref, key_pages, KEY_PAGE_GROUP)

    @pl.when(ph == kst)
    def _():
        lam = _lambda(lamv_ref[...], lambda_init)

        def soft(s_ref, s_self):
            s = s_ref[...]
            m = jnp.maximum(jnp.max(s, axis=-1, keepdims=True), s_self)
            p = jnp.exp(s - m)
            p_self = jnp.exp(s_self - m)
            inv = 1.0 / (jnp.sum(p, axis=-1, keepdims=True) + p_self)
            return p * inv, p_self * inv

        w0, w0s = soft(s0_ref, self_ref[:N_HEADS])
        w1, w1s = soft(s1_ref, self_ref[N_HEADS:])
        w_ref[...] = w0 - lam * w1
        wself_ref[...] = w0s - lam * w1s
        acc_ref[...] = jnp.zeros_like(acc_ref)

    @pl.when(ph >= kst)
    def _():
        lane = lax.broadcasted_iota(jnp.int32, (N_HEADS, PAGE_SIZE), 1)
        keep = (lane % N_HEADS) == lax.broadcasted_iota(jnp.int32, (N_HEADS, PAGE_SIZE), 0)
        per = PAGE_SIZE // N_HEADS
        parts = []

        def value_pages(slots):
            for i in slots:
                off = pl.multiple_of(((ph - kst) * npp + i) * PAGE_SIZE, PAGE_SIZE)
                w = w_ref[:, pl.ds(off, PAGE_SIZE)]
                wsel = jnp.concatenate(
                    [jnp.where(keep, jnp.take_along_axis(w, b * per + lane // N_HEADS, axis=1), 0.0)
                     for b in range(N_HEADS)], axis=1).astype(BF16)
                parts.append(jnp.dot(wsel, ring[i].astype(BF16), preferred_element_type=F32))

        consume(cv_ref, value_pages, VALUE_PAGE_GROUP)
        acc_ref[...] = acc_ref[...] + functools.reduce(lambda a, b: a + b, parts)

    @pl.when(ph == sps - 1)
    def _():
        o = acc_ref[...] + wself_ref[...] * vn_ref[seq_i]
        o_ref[seq_i] = _rms(o, subg_ref[...]) * (1.0 - lambda_init)


def kernel(x_prompt, x_sample, cache_k, cache_v, page_table, norm1_g, w_in, lam_q1, lam_k1, lam_q2, lam_k2,
           subln_g, gm_ln_g, gm_ln_b, gm_ws, gm_b, w_pa, w_pb, w_o, norm2_g, w_up, w_down, final_g):
    batch, seq, _ = x_prompt.shape
    nb = x_sample.shape[0]
    depth = w_in.shape[0]
    n_pool = cache_k.shape[1]
    xp = x_prompt.reshape(batch * seq, D_MODEL)
    xs = x_sample.reshape(nb, D_MODEL)
    ckt = jnp.transpose(cache_k, (0, 1, 3, 4, 5, 2)).reshape(depth, n_pool, D_MODEL, PAGE_SIZE)
    cvf = cache_v.reshape(depth, n_pool, PAGE_SIZE * N_HEADS, V_DIM)
    rowv = lambda a: a.reshape(1, -1)
    gf = rowv(final_g)

    kt_all = v_all = None
    ksm, vsm, gvs = [], [], []
    for l in range(depth):
        lambda_init = 0.8 - 0.6 * math.exp(-0.3 * l)
        last = l == depth - 1
        win = w_in[l].astype(BF16)
        wpa, wpb, wo = w_pa[l].astype(BF16), w_pb[l].astype(BF16), w_o[l].astype(BF16)
        wup, wdn = w_up[l].astype(BF16), w_down[l].astype(BF16)
        g1, g2 = rowv(norm1_g[l]), rowv(norm2_g[l])
        lng, lnb = rowv(gm_ln_g[l]), rowv(gm_ln_b[l])
        lamv = jnp.stack([lam_q1[l], lam_k1[l], lam_q2[l], lam_k2[l]])
        subg = rowv(subln_g[l])

        bs_full = jnp.broadcast_to(gm_b[l][:, :, None], (GM_GROUPS, CHUNK, CHUNK))
        qt, kt_all, kb, v_all, vt, b_in, ga, gb = _inproj_prompt(
            xp, g1, win, lng, lnb, gm_ws[l], bs_full, kt_all, v_all,
            layer=l, depth=depth, batch=batch, seq=seq)
        subg_tile = jnp.broadcast_to(subln_g[l][:, None], (V_DIM, V_DIM))
        ws_row = rowv(jnp.repeat(gm_ws[l][:, 0, 0], CHUNK))
        bs_row = rowv(jnp.repeat(gm_b[l][:, 0], CHUNK))
        qs_, kfs, vfs, b_in_s, ga_s, gb_s, gv = _inproj_decode(xs, g1, win, lng, lnb, ws_row, bs_row)

        a_in, a_in_s = _attention(qt, kb, vt, lamv, subg_tile, qs_, kfs, vfs, subg, ckt, cvf, page_table,
                                  batch=batch, seq=seq, layer=l, lambda_init=lambda_init)
        xp = _output(xp, a_in, b_in, ga, gb, wpa, wpb, wo, g2, wup, wdn, gf, final=last, tm=ROW_TILE)
        xs = _output(xs, a_in_s.reshape(nb, D_MODEL).astype(BF16), b_in_s, ga_s, gb_s, wpa, wpb, wo, g2, wup, wdn,
                     gf, final=last, tm=nb)
        ksm.append(kfs)
        vsm.append(vfs)
        gvs.append(gv)

    y_prompt = xp.reshape(batch, seq, D_MODEL)
    y_sample = xs.reshape(nb, 1, D_MODEL)
    k_prompt = jnp.transpose(kt_all.reshape(depth, batch, N_HEADS, 2, HEAD_DIM, seq), (0, 1, 5, 2, 3, 4))
    k_sample = jnp.stack(ksm).reshape(depth, nb, 1, N_HEADS, 2, HEAD_DIM)
    v_sample = jnp.stack(vsm).reshape(depth, nb, 1, N_HEADS, V_DIM)
    gv_sample = jnp.stack(gvs).reshape(depth, nb, 1, D_MODEL)
    return y_prompt, y_sample, k_prompt, v_all, k_sample, v_sample, gv_sample
```

```python
import functools
import math

import jax
import jax.numpy as jnp
from jax import lax
from jax.experimental import pallas as pl
from jax.experimental.pallas import tpu as pltpu

D_MODEL = 1024
N_HEADS = 8
HEAD_DIM = 64
V_DIM = 128
GM_GROUPS = 8
CHUNK = 128
PAGE_SIZE = 128
EPS = 1e-6

F32 = jnp.float32
BF16 = jnp.bfloat16

ROW_TILE = 256
Q_TILE = 512
K_TILE = 256
HEADS_PER_STEP = 2
AUG = 16
KEY_PAGE_GROUP = 4
VALUE_PAGE_GROUP = 8
LOG2E = math.log2(math.e)
VMEM_LIMIT = 56 * 1024 * 1024


def _const_spec(shape):
    nd = len(shape)
    return pl.BlockSpec(shape, lambda *_: (0,) * nd, pipeline_mode=pl.Buffered(1))


def _params(n_axes):
    return pltpu.CompilerParams(dimension_semantics=("arbitrary",) * n_axes, vmem_limit_bytes=VMEM_LIMIT)


def _gelu_tanh(x):
    c = math.sqrt(2.0 / math.pi)
    return 0.5 * x * (1.0 + jnp.tanh(c * (x + 0.044715 * (x * x * x))))


def _rms(x, g):
    return x * lax.rsqrt(jnp.mean(x * x, axis=-1, keepdims=True) + EPS) * g


def _lambda(lamv, lambda_init):
    s1 = jnp.sum(lamv[0:1] * lamv[1:2], axis=-1, keepdims=True)
    s2 = jnp.sum(lamv[2:3] * lamv[3:4], axis=-1, keepdims=True)
    return jnp.exp(s1) - jnp.exp(s2) + lambda_init


def _inproj_kernel(decode, n_alias, x_ref, g1_ref, win_ref, lng_ref, lnb_ref, ws_ref, bs_ref, *rest):
    outs = rest[n_alias:]
    if decode:
        q_ref, kf_ref, vf_ref, bin_ref, ga_ref, gb_ref, gv_ref = outs
    else:
        qt_ref, kt_ref, kb_ref, vf_ref, vt_ref, bin_ref, ga_ref, gb_ref = outs
    rows = x_ref.shape[0]
    h = _rms(x_ref[...], g1_ref[...]).astype(BF16)

    def proj(c):
        return jnp.dot(h, win_ref[:, c * D_MODEL:(c + 1) * D_MODEL], preferred_element_type=F32)

    zv = _gelu_tanh(proj(4))
    xc = zv - jnp.mean(zv, axis=-1, keepdims=True)
    vg = xc * lax.rsqrt(jnp.mean(xc * xc, axis=-1, keepdims=True) + EPS) * lng_ref[...] + lnb_ref[...]
    u = _gelu_tanh(proj(3))

    q = proj(0)
    k = proj(1)
    v = proj(2)
    if decode:
        q_ref[...] = q
        kf_ref[...] = k
    else:
        qt_ref[0] = (q * (LOG2E * HEAD_DIM ** -0.5)).T.astype(BF16)
        kt_ref[0, 0] = k.T
        kb_ref[...] = k.astype(BF16)
        vt_ref[0] = v.T.astype(BF16)
    v_tiles = v.reshape(rows, N_HEADS, V_DIM)
    if decode:
        vf_ref[...] = v_tiles
    else:
        vf_ref[0, 0] = v_tiles

    if decode:
        gv_ref[...] = vg
        bin_ref[...] = (u * (ws_ref[...] * vg + bs_ref[...])).astype(BF16)
    else:
        vgb = vg.astype(BF16)
        tril = (lax.broadcasted_iota(jnp.int32, (CHUNK, CHUNK), 0)
                >= lax.broadcasted_iota(jnp.int32, (CHUNK, CHUNK), 1))
        for g in range(GM_GROUPS):
            wm = jnp.where(tril, ws_ref[g], 0.0).astype(BF16)
            cols = slice(g * CHUNK, (g + 1) * CHUNK)
            for c in range(rows // CHUNK):
                rs = slice(c * CHUNK, (c + 1) * CHUNK)
                s = jnp.dot(wm, vgb[rs, cols], preferred_element_type=F32) + bs_ref[g]
                bin_ref[rs, cols] = (u[rs, cols] * s).astype(BF16)

    ga_ref[...] = jax.nn.sigmoid(proj(5)).astype(BF16)
    gb_ref[...] = jax.nn.sigmoid(proj(6)).astype(BF16)


def _inproj_prompt(x, g1, win, lng, lnb, ws, bs, kt_all, v_all, *, layer, depth, batch, seq):
    n = batch * seq
    tm = ROW_TILE
    nt = seq // tm
    row = jax.ShapeDtypeStruct((n, D_MODEL), BF16)
    fmaj = jax.ShapeDtypeStruct((batch, D_MODEL, seq), BF16)
    out_shape = [fmaj,
                 jax.ShapeDtypeStruct((depth, batch, D_MODEL, seq), F32),
                 row,
                 jax.ShapeDtypeStruct((depth, batch, seq, N_HEADS, V_DIM), F32),
                 fmaj, row, row, row]
    tile = pl.BlockSpec((tm, D_MODEL), lambda b, i: (b * nt + i, 0))
    ftile = pl.BlockSpec((1, D_MODEL, tm), lambda b, i: (b, 0, i))
    out_specs = [ftile,
                 pl.BlockSpec((1, 1, D_MODEL, tm), lambda b, i: (layer, b, 0, i)),
                 tile,
                 pl.BlockSpec((1, 1, tm, N_HEADS, V_DIM), lambda b, i: (layer, b, i, 0, 0)),
                 ftile, tile, tile, tile]
    consts = [g1, win, lng, lnb, ws, bs]
    in_specs = [tile] + [_const_spec(c.shape) for c in consts]
    args = [x] + consts
    aliases = {}
    n_alias = 0
    if kt_all is not None:
        in_specs += [pl.BlockSpec(memory_space=pl.ANY)] * 2
        aliases = {len(args): 1, len(args) + 1: 3}
        args += [kt_all, v_all]
        n_alias = 2
    return pl.pallas_call(
        functools.partial(_inproj_kernel, False, n_alias),
        out_shape=out_shape, grid=(batch, nt), in_specs=in_specs, out_specs=out_specs,
        input_output_aliases=aliases, compiler_params=_params(2), name="inproj_prompt",
    )(*args)


def _inproj_decode(x, g1, win, lng, lnb, ws, bs):
    n = x.shape[0]
    row = lambda dt: jax.ShapeDtypeStruct((n, D_MODEL), dt)
    out_shape = [row(F32), row(F32), jax.ShapeDtypeStruct((n, N_HEADS, V_DIM), F32),
                 row(BF16), row(BF16), row(BF16), row(F32)]
    consts = [g1, win, lng, lnb, ws, bs]
    full = lambda s: pl.BlockSpec(s.shape, lambda i: (0,) * len(s.shape))
    return pl.pallas_call(
        functools.partial(_inproj_kernel, True, 0),
        out_shape=out_shape, grid=(1,),
        in_specs=[full(x)] + [_const_spec(c.shape) for c in consts],
        out_specs=[full(s) for s in out_shape],
        compiler_params=_params(1), name="inproj_decode",
    )(x, *consts)


def _output_kernel(final, a_fmajor, x_ref, a_ref, b_ref, ga_ref, gb_ref, wpa_ref, wpb_ref, wo_ref, g2_ref,
                   wup_ref, wdn_ref, gf_ref, o_ref):
    if a_fmajor:
        a = lax.dot_general(a_ref[0], wpa_ref[...], (((0,), (0,)), ((), ())), preferred_element_type=F32)
    else:
        a = jnp.dot(a_ref[...], wpa_ref[...], preferred_element_type=F32)
    b = jnp.dot(b_ref[...], wpb_ref[...], preferred_element_type=F32)
    mix = ga_ref[...].astype(F32) * a + gb_ref[...].astype(F32) * b
    x = x_ref[...] + jnp.dot(mix.astype(BF16), wo_ref[...], preferred_element_type=F32)
    h2 = _rms(x, g2_ref[...]).astype(BF16)
    hid = jnp.maximum(jnp.dot(h2, wup_ref[...], preferred_element_type=F32), 0.0)
    x = x + jnp.dot((hid * hid).astype(BF16), wdn_ref[...], preferred_element_type=F32)
    o_ref[...] = _rms(x, gf_ref[...]) if final else x


def _output(x, a_in, b_in, ga, gb, wpa, wpb, wo, g2, wup, wdn, gf, *, final, tm):
    n = x.shape[0]
    tile = pl.BlockSpec((tm, D_MODEL), lambda i: (i, 0))
    a_fmajor = a_in.ndim == 3
    if a_fmajor:
        nt = a_in.shape[2] // tm
        a_spec = pl.BlockSpec((1, D_MODEL, tm), lambda i: (i // nt, 0, i % nt))
    else:
        a_spec = tile
    consts = [wpa, wpb, wo, g2, wup, wdn, gf]
    return pl.pallas_call(
        functools.partial(_output_kernel, final, a_fmajor),
        out_shape=jax.ShapeDtypeStruct((n, D_MODEL), F32),
        grid=(n // tm,),
        in_specs=[tile, a_spec, tile, tile, tile] + [_const_spec(c.shape) for c in consts],
        out_specs=tile, compiler_params=_params(1), name="output_block",
    )(x, a_in, b_in, ga, gb, *consts)


def _attention_kernel(lambda_init, sample_cfg, pt_ref, q_ref, k_ref, vt_ref, lamv_ref, subg_ref, *rest):
    n_sample_in = 6
    sample_in = rest[:n_sample_in]
    o_ref, os_ref = rest[n_sample_in:n_sample_in + 2]
    sa_scr, sb_scr, m_scr, l_scr, acc_scr = rest[n_sample_in + 2:n_sample_in + 7]
    sample_scr = rest[n_sample_in + 7:]
    hp = pl.program_id(1)
    qi = pl.program_id(2)
    step = (pl.program_id(0) * pl.num_programs(1) + hp) * pl.num_programs(2) + qi
    tq, tk = Q_TILE, K_TILE
    q0 = qi * tq
    kcol = lax.broadcasted_iota(jnp.int32, (tk, V_DIM), 1)
    ramp = jnp.where(kcol < 2, lax.broadcasted_iota(jnp.int32, (tk, V_DIM), 0), 0).astype(F32).astype(BF16)
    ones = jnp.ones((AUG, tk), BF16)
    feat = lax.broadcasted_iota(jnp.int32, (V_DIM, tq), 0)

    chains = []
    for hh in range(HEADS_PER_STEP):
        hd = hp * HEADS_PER_STEP + hh
        qt = q_ref[0, hh * V_DIM:(hh + 1) * V_DIM, :]
        slope = jnp.exp2(-(hd + 1).astype(F32) * jnp.ones((1, tq), F32)) * LOG2E
        s_hi = slope.astype(BF16).astype(F32)
        s_lo = (slope - s_hi).astype(BF16).astype(F32)
        aug_rows = jnp.where(feat == 0, s_hi, jnp.where(feat == 1, s_lo, 0.0)).astype(BF16)
        for mp in range(2):
            keep = (feat < HEAD_DIM) if mp == 0 else (feat >= HEAD_DIM)
            w = jnp.concatenate([jnp.where(keep, qt, jnp.zeros_like(qt)), aug_rows], axis=0)
            chains.append((hh, w, s_hi[:, :1] + s_lo[:, :1]))

    nch = len(chains)

    all_q = slice(0, tq)
    late_q = slice(tk, tq)

    def scores(j, ci, buf, cols=all_q):
        hh, w, _ = chains[ci]
        k0 = pl.multiple_of(j * tk, tk)
        kaug = jnp.concatenate([k_ref[pl.ds(k0, tk), hh * V_DIM:(hh + 1) * V_DIM], ramp], axis=1)
        buf[ci, :, cols] = jnp.dot(kaug, w[:, cols], preferred_element_type=F32)

    def update(j, ci, buf, masked, cols=all_q):
        hh, _, slope_v = chains[ci]
        s = buf[ci, :, cols]
        nq = cols.stop - cols.start
        k0 = pl.multiple_of(j * tk, tk)
        c = slope_v * (k0 - q0).astype(F32)
        if masked:
            kk = lax.broadcasted_iota(jnp.int32, (tk, nq), 0) + (k0 - q0)
            qq = lax.broadcasted_iota(jnp.int32, (tk, nq), 1) + cols.start
            s = jnp.where(qq >= kk, s, -jnp.inf)
        m = m_scr[ci, :, cols]
        m_new = jnp.maximum(m, jnp.max(s, axis=0, keepdims=True) + c)
        alpha = jnp.exp2(m - m_new)
        m_scr[ci, :, cols] = m_new
        p = jnp.exp2(s - (m_new - c)).astype(BF16)
        vaug = jnp.concatenate([vt_ref[0, hh * V_DIM:(hh + 1) * V_DIM, pl.ds(k0, tk)], ones], axis=0)
        pv = jnp.dot(vaug, p, preferred_element_type=F32)
        l_scr[ci, :, cols] = alpha * l_scr[ci, :, cols] + pv[V_DIM:V_DIM + 8]
        acc_scr[ci, :, cols] = alpha * acc_scr[ci, :, cols] + pv[:V_DIM]

    def stage(j_next, buf_next, j, buf, masked, cols_next=all_q, cols=all_q):
        if j_next is not None:
            for ci in range(nch):
                scores(j_next, ci, buf_next, cols_next)
        for ci in range(nch):
            update(j, ci, buf, masked, cols)

    def body(t, carry):
        stage(2 * t + 1, sb_scr, 2 * t, sa_scr, False)
        stage(2 * t + 2, sa_scr, 2 * t + 1, sb_scr, False)
        return carry

    m_scr[...] = jnp.full(m_scr.shape, -jnp.inf, F32)
    l_scr[...] = jnp.zeros(l_scr.shape, F32)
    acc_scr[...] = jnp.zeros(acc_scr.shape, F32)
    for ci in range(nch):
        scores(0, ci, sa_scr)
    _sample_stream(lambda_init, sample_cfg, step, pt_ref, lamv_ref, sample_in, os_ref, sample_scr)
    lax.fori_loop(0, qi, body, 0)
    stage(2 * qi + 1, sb_scr, 2 * qi, sa_scr, True, cols_next=late_q)
    stage(None, None, 2 * qi + 1, sb_scr, True, cols=late_q)

    lam = _lambda(lamv_ref[...], lambda_init)
    subg = jnp.concatenate([subg_ref[...]] * (tq // V_DIM), axis=1)
    for hh in range(HEADS_PER_STEP):
        o0 = acc_scr[2 * hh] / l_scr[2 * hh, 0:1]
        o1 = acc_scr[2 * hh + 1] / l_scr[2 * hh + 1, 0:1]
        ot = o0 - lam * o1
        ot = ot * lax.rsqrt(jnp.mean(ot * ot, axis=0, keepdims=True) + EPS) * subg
        o_ref[0, hh * V_DIM:(hh + 1) * V_DIM, :] = (ot * (1.0 - lambda_init)).astype(BF16)


def _attention(qt, kb, vt, lamv, subg_tile, qs, kn, vn, subg_row, cache_kt, cache_vf, page_table, *,
               batch, seq, layer, lambda_init):
    assert Q_TILE == 2 * K_TILE and seq % Q_TILE == 0
    hps = HEADS_PER_STEP
    grid = (batch, N_HEADS // hps, seq // Q_TILE)
    n_steps = grid[0] * grid[1] * grid[2]
    nb, n_pages = page_table.shape
    cfg = _SampleCfg(layer, nb, n_pages, n_steps)
    past = n_pages * PAGE_SIZE
    fspec = pl.BlockSpec((1, hps * V_DIM, Q_TILE), lambda b, h, i, pt: (b, h, i))
    whole = lambda a: pl.BlockSpec(a.shape, lambda b, h, i, pt: (0,) * a.ndim, pipeline_mode=pl.Buffered(1))
    hbm = pl.BlockSpec(memory_space=pl.ANY)
    nch = 2 * hps
    sample_out = jax.ShapeDtypeStruct((nb, N_HEADS, V_DIM), F32)
    grid_spec = pltpu.PrefetchScalarGridSpec(
        num_scalar_prefetch=1, grid=grid,
        in_specs=[fspec,
                  pl.BlockSpec((seq, hps * V_DIM), lambda b, h, i, pt: (b, h)),
                  pl.BlockSpec((1, hps * V_DIM, seq), lambda b, h, i, pt: (b, h, 0)),
                  whole(lamv), whole(subg_tile),
                  whole(qs), whole(kn), whole(vn), whole(subg_row), hbm, hbm],
        out_specs=[fspec, pl.BlockSpec(sample_out.shape, lambda b, h, i, pt: (0, 0, 0))],
        scratch_shapes=[pltpu.VMEM((nch, K_TILE, Q_TILE), F32), pltpu.VMEM((nch, K_TILE, Q_TILE), F32),
                        pltpu.VMEM((nch, 1, Q_TILE), F32), pltpu.VMEM((nch, 8, Q_TILE), F32),
                        pltpu.VMEM((nch, V_DIM, Q_TILE), F32),
                        pltpu.VMEM((cfg.pages_per_step, D_MODEL, PAGE_SIZE), F32),
                        pltpu.SemaphoreType.DMA((cfg.pages_per_step,)),
                        pltpu.VMEM((D_MODEL, PAGE_SIZE), F32),
                        pltpu.VMEM((N_HEADS, past), F32),
                        pltpu.VMEM((N_HEADS, past), F32),
                        pltpu.VMEM((2 * N_HEADS, 1), F32),
                        pltpu.VMEM((N_HEADS, past), F32),
                        pltpu.VMEM((N_HEADS, 1), F32),
                        pltpu.VMEM((N_HEADS, V_DIM), F32)],
    )
    return pl.pallas_call(
        functools.partial(_attention_kernel, lambda_init, cfg),
        out_shape=[jax.ShapeDtypeStruct(qt.shape, BF16), sample_out],
        grid_spec=grid_spec, compiler_params=_params(3), name="attention",
    )(page_table, qt, kb, vt, lamv, subg_tile, qs, kn, vn, subg_row, cache_kt, cache_vf)


class _SampleCfg:
    def __init__(self, layer, nb, n_pages, n_steps):
        assert n_steps % (2 * nb) == 0 and n_pages % (n_steps // nb // 2) == 0
        self.layer, self.nb, self.n_pages, self.n_steps = layer, nb, n_pages, n_steps
        self.steps_per_seq = n_steps // nb
        self.key_steps = self.steps_per_seq // 2
        self.pages_per_step = n_pages // self.key_steps
        assert self.pages_per_step % KEY_PAGE_GROUP == 0 and self.pages_per_step % VALUE_PAGE_GROUP == 0


def _sample_stream(lambda_init, cfg, step, pt_ref, lamv_ref, sample_in, o_ref, scratch):
    q_ref, kn_ref, vn_ref, subg_ref, ck_ref, cv_ref = sample_in
    ring, sems, qcol_ref, s0_ref, s1_ref, self_ref, w_ref, wself_ref, acc_ref = scratch
    sps, kst, npp = cfg.steps_per_seq, cfg.key_steps, cfg.pages_per_step
    past = cfg.n_pages * PAGE_SIZE
    seq_i = step // sps
    ph = step % sps

    def page_copy(src_ref, st, i):
        page = pt_ref[st // sps, ((st % sps) % kst) * npp + i]
        return pltpu.make_async_copy(src_ref.at[cfg.layer, page], ring.at[i], sems.at[i])

    def start_pages(st, slots, extra=True):
        keys = (st % sps) < kst

        @pl.when(jnp.logical_and(extra, keys))
        def _():
            for i in slots:
                page_copy(ck_ref, st, i).start()

        @pl.when(jnp.logical_and(extra, jnp.logical_not(keys)))
        def _():
            for i in slots:
                page_copy(cv_ref, st, i).start()

    def consume(src_ref, group_fn, group):
        for g0 in range(0, npp, group):
            slots = range(g0, g0 + group)
            for i in slots:
                page_copy(src_ref, step, i).wait()
            group_fn(slots)
            start_pages(step + 1, slots, step + 1 < cfg.n_steps)

    @pl.when(step == 0)
    def _():
        start_pages(step, range(npp))

    @pl.when(ph == 0)
    def _():
        q = q_ref[pl.ds(seq_i, 1), :] * (HEAD_DIM ** -0.5)
        qcol_ref[...] = jnp.broadcast_to(q, (V_DIM, D_MODEL)).T
        row = lax.broadcasted_iota(jnp.int32, (2 * N_HEADS, D_MODEL), 0)
        col = lax.broadcasted_iota(jnp.int32, (2 * N_HEADS, D_MODEL), 1)
        own = (col // V_DIM == row % N_HEADS) & ((col // HEAD_DIM) % 2 == row // N_HEADS)
        prod = jnp.broadcast_to(q * kn_ref[pl.ds(seq_i, 1), :], (2 * N_HEADS, D_MODEL))
        self_ref[...] = jnp.sum(jnp.where(own, prod, 0.0), axis=-1, keepdims=True)

    @pl.when(ph < kst)
    def _():
        def key_pages(slots):
            dist = [((ph * npp + i) * PAGE_SIZE - past
                     + lax.broadcasted_iota(jnp.int32, (1, PAGE_SIZE), 1)).astype(F32) for i in slots]
            for seg in range(2 * N_HEADS):
                hd, mp = divmod(seg, 2)
                rows = slice(seg * HEAD_DIM, (seg + 1) * HEAD_DIM)
                qc = qcol_ref[rows, :]
                s_ref = s1_ref if mp else s0_ref
                for n, i in enumerate(slots):
                    sc = jnp.sum(ring[i, rows, :] * qc, axis=0, keepdims=True)
                    off = pl.multiple_of((ph * npp + i) * PAGE_SIZE, PAGE_SIZE)
                    s_ref[hd:hd + 1, pl.ds(off, PAGE_SIZE)] = sc + 2.0 ** -(hd + 1) * dist[n]

        consume(ck_ref, key_pages, KEY_PAGE_GROUP)

    @pl.when(ph == kst)
    def _():
        lam = _lambda(lamv_ref[...], lambda_init)

        def soft(s_ref, s_self):
            s = s_ref[...]
            m = jnp.maximum(jnp.max(s, axis=-1, keepdims=True), s_self)
            p = jnp.exp(s - m)
            p_self = jnp.exp(s_self - m)
            inv = 1.0 / (jnp.sum(p, axis=-1, keepdims=True) + p_self)
            return p * inv, p_self * inv

        w0, w0s = soft(s0_ref, self_ref[:N_HEADS])
        w1, w1s = soft(s1_ref, self_ref[N_HEADS:])
        w_ref[...] = w0 - lam * w1
        wself_ref[...] = w0s - lam * w1s
        acc_ref[...] = jnp.zeros_like(acc_ref)

    @pl.when(ph >= kst)
    def _():
        lane = lax.broadcasted_iota(jnp.int32, (N_HEADS, PAGE_SIZE), 1)
        keep = (lane % N_HEADS) == lax.broadcasted_iota(jnp.int32, (N_HEADS, PAGE_SIZE), 0)
        per = PAGE_SIZE // N_HEADS
        parts = []

        def value_pages(slots):
            for i in slots:
                off = pl.multiple_of(((ph - kst) * npp + i) * PAGE_SIZE, PAGE_SIZE)
                w = w_ref[:, pl.ds(off, PAGE_SIZE)]
                wsel = jnp.concatenate(
                    [jnp.where(keep, jnp.take_along_axis(w, b * per + lane // N_HEADS, axis=1), 0.0)
                     for b in range(N_HEADS)], axis=1).astype(BF16)
                parts.append(jnp.dot(wsel, ring[i].astype(BF16), preferred_element_type=F32))

        consume(cv_ref, value_pages, VALUE_PAGE_GROUP)
        acc_ref[...] = acc_ref[...] + functools.reduce(lambda a, b: a + b, parts)

    @pl.when(ph == sps - 1)
    def _():
        o = acc_ref[...] + wself_ref[...] * vn_ref[seq_i]
        o_ref[seq_i] = _rms(o, subg_ref[...]) * (1.0 - lambda_init)


def kernel(x_prompt, x_sample, cache_k, cache_v, page_table, norm1_g, w_in, lam_q1, lam_k1, lam_q2, lam_k2,
           subln_g, gm_ln_g, gm_ln_b, gm_ws, gm_b, w_pa, w_pb, w_o, norm2_g, w_up, w_down, final_g):
    batch, seq, _ = x_prompt.shape
    nb = x_sample.shape[0]
    depth = w_in.shape[0]
    n_pool = cache_k.shape[1]
    xp = x_prompt.reshape(batch * seq, D_MODEL)
    xs = x_sample.reshape(nb, D_MODEL)
    ckt = jnp.transpose(cache_k, (0, 1, 3, 4, 5, 2)).reshape(depth, n_pool, D_MODEL, PAGE_SIZE)
    cvf = cache_v.reshape(depth, n_pool, PAGE_SIZE * N_HEADS, V_DIM)
    rowv = lambda a: a.reshape(1, -1)
    gf = rowv(final_g)

    kt_all = v_all = None
    ksm, vsm, gvs = [], [], []
    for l in range(depth):
        lambda_init = 0.8 - 0.6 * math.exp(-0.3 * l)
        last = l == depth - 1
        win = w_in[l].astype(BF16)
        wpa, wpb, wo = w_pa[l].astype(BF16), w_pb[l].astype(BF16), w_o[l].astype(BF16)
        wup, wdn = w_up[l].astype(BF16), w_down[l].astype(BF16)
        g1, g2 = rowv(norm1_g[l]), rowv(norm2_g[l])
        lng, lnb = rowv(gm_ln_g[l]), rowv(gm_ln_b[l])
        lamv = jnp.stack([lam_q1[l], lam_k1[l], lam_q2[l], lam_k2[l]])
        subg = rowv(subln_g[l])

        bs_full = jnp.broadcast_to(gm_b[l][:, :, None], (GM_GROUPS, CHUNK, CHUNK))
        qt, kt_all, kb, v_all, vt, b_in, ga, gb = _inproj_prompt(
            xp, g1, win, lng, lnb, gm_ws[l], bs_full, kt_all, v_all,
            layer=l, depth=depth, batch=batch, seq=seq)
        subg_tile = jnp.broadcast_to(subln_g[l][:, None], (V_DIM, V_DIM))
        ws_row = rowv(jnp.repeat(gm_ws[l][:, 0, 0], CHUNK))
        bs_row = rowv(jnp.repeat(gm_b[l][:, 0], CHUNK))
        qs_, kfs, vfs, b_in_s, ga_s, gb_s, gv = _inproj_decode(xs, g1, win, lng, lnb, ws_row, bs_row)

        a_in, a_in_s = _attention(qt, kb, vt, lamv, subg_tile, qs_, kfs, vfs, subg, ckt, cvf, page_table,
                                  batch=batch, seq=seq, layer=l, lambda_init=lambda_init)
        xp = _output(xp, a_in, b_in, ga, gb, wpa, wpb, wo, g2, wup, wdn, gf, final=last, tm=ROW_TILE)
        xs = _output(xs, a_in_s.reshape(nb, D_MODEL).astype(BF16), b_in_s, ga_s, gb_s, wpa, wpb, wo, g2, wup, wdn,
                     gf, final=last, tm=nb)
        ksm.append(kfs)
        vsm.append(vfs)
        gvs.append(gv)

    y_prompt = xp.reshape(batch, seq, D_MODEL)
    y_sample = xs.reshape(nb, 1, D_MODEL)
    k_prompt = jnp.transpose(kt_all.reshape(depth, batch, N_HEADS, 2, HEAD_DIM, seq), (0, 1, 5, 2, 3, 4))
    k_sample = jnp.stack(ksm).reshape(depth, nb, 1, N_HEADS, 2, HEAD_DIM)
    v_sample = jnp.stack(vsm).reshape(depth, nb, 1, N_HEADS, V_DIM)
    gv_sample = jnp.stack(gvs).reshape(depth, nb, 1, D_MODEL)
    return y_prompt, y_sample, k_prompt, v_all, k_sample, v_sample, gv_sample
```
